```python
import jax, jax.numpy as jnp
from jax import lax
import numpy as np

D_MODEL = 1024
BATCH = 2
SEQ = 16384
DEPTH = 1
DEC_BATCH = 16
DEC_SEQ = 2048
PAST_LEN = 128

CHUNK = 128
E_A = D_MODEL
A_GROUPS = 8
A_GROUP_W = E_A // A_GROUPS
E_B = D_MODEL
POOL_WINDOWS = (2, 4, 8, 16)
POOL_GROUPS = len(POOL_WINDOWS)
POOL_GROUP_W = E_B // POOL_GROUPS
IN_WIDTHS = (E_A, E_A, E_A, E_B, E_B, D_MODEL, D_MODEL)
IN_TOTAL = sum(IN_WIDTHS)
SPLIT_POINTS = [int(s) for s in np.cumsum(IN_WIDTHS)[:-1]]
DEEPNORM_ALPHA = (2.0 * DEPTH) ** 0.25
DEEPNORM_BETA = (8.0 * DEPTH) ** -0.25
LN_EPS = 1e-5

kernel_name = "gated_spatial_pool_hybrid_encoder"


def _layernorm(x, g, b):
    xf = x.astype(jnp.float32)
    mu = jnp.mean(xf, axis=-1, keepdims=True)
    xc = xf - mu
    var = jnp.mean(xc * xc, axis=-1, keepdims=True)
    return (xc * lax.rsqrt(var + LN_EPS) * g.astype(jnp.float32) + b.astype(jnp.float32)).astype(x.dtype)


def _centred_pool_minus_self(p, window):
    b, s, c = p.shape
    pf = p.astype(jnp.float32)
    csum = jnp.concatenate([jnp.zeros((b, 1, c), jnp.float32), jnp.cumsum(pf, axis=1)], axis=1)
    t = jnp.arange(s)
    lo = jnp.clip(t - window // 2, 0, s)
    hi = jnp.clip(t + window - window // 2, 0, s)
    sums = jnp.take(csum, hi, axis=1) - jnp.take(csum, lo, axis=1)
    cnt = (hi - lo).astype(jnp.float32)[None, :, None]
    return (sums / cnt - pf).astype(p.dtype)


def _layer(x, w_in, b_in, ln_v_g, ln_v_b, w_spatial, b_spatial, w_pool, b_pool, pool_scale,
           w_br_a, w_br_b, w_out, b_out, ln_g, ln_b):
    bsz, s, _ = x.shape
    h = jnp.einsum('bsd,de->bse', x, w_in) + b_in
    u, v, z_a, p, z_b, g_a, g_b = jnp.split(h, SPLIT_POINTS, axis=-1)

    v = _layernorm(v, ln_v_g, ln_v_b)
    v = v.reshape(bsz, s // CHUNK, CHUNK, A_GROUPS, A_GROUP_W)
    v = jnp.einsum('bnpgc,gqp->bnqgc', v, w_spatial) + b_spatial.T[None, None, :, :, None]
    br_a = u * v.reshape(bsz, s, E_A) * jax.nn.silu(z_a)

    p = p.reshape(bsz, s, POOL_GROUPS, POOL_GROUP_W)
    pooled = jnp.stack([_centred_pool_minus_self(p[:, :, i, :], w) for i, w in enumerate(POOL_WINDOWS)], axis=2)
    m = jnp.einsum('bsgc,gcd->bsgd', pooled, w_pool) + b_pool
    br_b = m.reshape(bsz, s, E_B) * pool_scale * jax.nn.silu(z_b)

    merged = (jax.nn.sigmoid(g_a) * jnp.einsum('bse,ed->bsd', br_a, w_br_a)
              + jax.nn.sigmoid(g_b) * jnp.einsum('bse,ed->bsd', br_b, w_br_b))
    out = jnp.einsum('bsd,de->bse', merged, w_out) + b_out
    return _layernorm(DEEPNORM_ALPHA * x + out, ln_g, ln_b)


def _trunk(x, w_in, b_in, ln_v_g, ln_v_b, w_spatial, b_spatial, w_pool, b_pool, pool_scale,
           w_br_a, w_br_b, w_out, b_out, ln_g, ln_b):
    for l in range(DEPTH):
        x = _layer(x, w_in[l], b_in[l], ln_v_g[l], ln_v_b[l], w_spatial[l], b_spatial[l], w_pool[l],
                   b_pool[l], pool_scale[l], w_br_a[l], w_br_b[l], w_out[l], b_out[l], ln_g[l], ln_b[l])
    return x


def setup_inputs(seed: int = 0) -> dict:
    key = jax.random.key(seed)
    ks = jax.random.split(key, 20)
    n = jax.random.normal
    f32 = jnp.float32
    return {
        "x_prompt": n(ks[0], (BATCH, SEQ, D_MODEL), f32),
        "x_sample": n(ks[1], (DEC_BATCH, DEC_SEQ, D_MODEL), f32),
        "w_in": n(ks[2], (DEPTH, D_MODEL, IN_TOTAL), f32) * D_MODEL ** -0.5,
        "b_in": n(ks[3], (DEPTH, IN_TOTAL), f32) * 0.02,
        "ln_v_g": 1.0 + 0.05 * n(ks[4], (DEPTH, E_A), f32),
        "ln_v_b": 0.02 * n(ks[5], (DEPTH, E_A), f32),
        "w_spatial": n(ks[6], (DEPTH, A_GROUPS, CHUNK, CHUNK), f32) * CHUNK ** -0.5,
        "b_spatial": 1.0 + 0.05 * n(ks[7], (DEPTH, A_GROUPS, CHUNK), f32),
        "w_pool": n(ks[8], (DEPTH, POOL_GROUPS, POOL_GROUP_W, POOL_GROUP_W), f32) * POOL_GROUP_W ** -0.5,
        "b_pool": 0.02 * n(ks[9], (DEPTH, POOL_GROUPS, POOL_GROUP_W), f32),
        "pool_scale": 1.0 + 0.05 * n(ks[10], (DEPTH, E_B), f32),
        "w_br_a": n(ks[11], (DEPTH, E_A, D_MODEL), f32) * (E_A ** -0.5 * DEEPNORM_BETA),
        "w_br_b": n(ks[12], (DEPTH, E_B, D_MODEL), f32) * (E_B ** -0.5 * DEEPNORM_BETA),
        "w_out": n(ks[13], (DEPTH, D_MODEL, D_MODEL), f32) * (D_MODEL ** -0.5 * DEEPNORM_BETA),
        "b_out": 0.02 * n(ks[14], (DEPTH, D_MODEL), f32),
        "ln_g": 1.0 + 0.05 * n(ks[15], (DEPTH, D_MODEL), f32),
        "ln_b": 0.02 * n(ks[16], (DEPTH, D_MODEL), f32),
    }


def reference(x_prompt, x_sample, w_in, b_in, ln_v_g, ln_v_b, w_spatial, b_spatial, w_pool, b_pool,
              pool_scale, w_br_a, w_br_b, w_out, b_out, ln_g, ln_b):
    y_prompt = _trunk(x_prompt, w_in, b_in, ln_v_g, ln_v_b, w_spatial, b_spatial, w_pool, b_pool,
                      pool_scale, w_br_a, w_br_b, w_out, b_out, ln_g, ln_b)
    y_sample = _trunk(x_sample, w_in, b_in, ln_v_g, ln_v_b, w_spatial, b_spatial, w_pool, b_pool,
                      pool_scale, w_br_a, w_br_b, w_out, b_out, ln_g, ln_b)
    return (y_prompt, y_sample)
```

```python
import functools

import jax
import jax.numpy as jnp
from jax.experimental import pallas as pl
from jax.experimental.pallas import tpu as pltpu

D_MODEL = 1024
CHUNK = 128
A_GROUPS = 8
A_GROUP_W = D_MODEL // A_GROUPS
POOL_WINDOWS = (2, 4, 8, 16)
POOL_GROUP_W = D_MODEL // len(POOL_WINDOWS)
DEPTH = 1
DEEPNORM_ALPHA = (2.0 * DEPTH) ** 0.25
LN_EPS = 1e-5

COL_U, COL_V, COL_ZA, COL_P, COL_ZB, COL_GA, COL_GB = (i * D_MODEL for i in range(7))

HALO = 16
TOKEN_TILE = 512
VMEM_LIMIT_BYTES = 58 * 1024 * 1024


def _layernorm(x, g, b):
    mu = jnp.mean(x, axis=-1, keepdims=True)
    xc = x - mu
    var = jnp.mean(xc * xc, axis=-1, keepdims=True)
    return xc * jax.lax.rsqrt(var + LN_EPS) * g + b


def _silu(z):
    return z * jax.nn.sigmoid(z)


def _layer_kernel(xprev_ref, x_ref, xnext_ref, w_in_ref, b_in_ref, ln_v_g_ref, ln_v_b_ref, w_sp_ref,
                  b_sp_ref, w_pool_ref, b_pool_ref, pool_scale_ref, w_br_a_ref, w_br_b_ref, w_out_ref,
                  b_out_ref, ln_g_ref, ln_b_ref, out_ref,
                  xext_ref, vn_ref, vs_ref, pext_ref, pooled_ref, m_ref, *, seq_len):
    tm = x_ref.shape[0]
    tiles_per_seq = seq_len // tm
    j = pl.program_id(0) % tiles_per_seq
    is_first = j == 0
    is_last = j == tiles_per_seq - 1

    def in_proj(lhs, col):
        return (jnp.dot(lhs, w_in_ref[:, col:col + D_MODEL], preferred_element_type=jnp.float32)
                + b_in_ref[:, col:col + D_MODEL])

    xext_ref[0:HALO, :] = xprev_ref[...].astype(jnp.bfloat16)
    xext_ref[HALO:HALO + tm, :] = x_ref[...].astype(jnp.bfloat16)
    xext_ref[HALO + tm:, :] = xnext_ref[...].astype(jnp.bfloat16)
    xb = xext_ref[HALO:HALO + tm, :]

    v = _layernorm(in_proj(xb, COL_V), ln_v_g_ref[...], ln_v_b_ref[...])
    vn_ref[...] = v.astype(jnp.bfloat16)
    for n in range(tm // CHUNK):
        rows = slice(n * CHUNK, (n + 1) * CHUNK)
        for g in range(A_GROUPS):
            cols = slice(g * A_GROUP_W, (g + 1) * A_GROUP_W)
            vs_ref[rows, cols] = (jnp.dot(w_sp_ref[g], vn_ref[rows, cols],
                                          preferred_element_type=jnp.float32) + b_sp_ref[:, cols])
    br_a = in_proj(xb, COL_U) * vs_ref[...] * _silu(in_proj(xb, COL_ZA))
    proj_a = jnp.dot(br_a.astype(jnp.bfloat16), w_br_a_ref[...], preferred_element_type=jnp.float32)
    merged = jax.nn.sigmoid(in_proj(xb, COL_GA)) * proj_a

    pext_ref[...] = in_proj(xext_ref[...], COL_P)
    pext_ref[0:HALO, :] = jnp.where(is_first, 0.0, pext_ref[0:HALO, :])
    pext_ref[HALO + tm:, :] = jnp.where(is_last, 0.0, pext_ref[HALO + tm:, :])
    t = j * tm + jax.lax.broadcasted_iota(jnp.int32, (tm, 1), 0)
    for g, w in enumerate(POOL_WINDOWS):
        cols = slice(g * POOL_GROUP_W, (g + 1) * POOL_GROUP_W)
        lo, hi = w // 2, w - w // 2
        acc = pext_ref[HALO - lo:HALO - lo + tm, cols]
        for d in range(-lo + 1, hi):
            acc = acc + pext_ref[HALO + d:HALO + d + tm, cols]
        cnt = jnp.minimum(t + hi, seq_len) - jnp.maximum(t - lo, 0)
        inv_cnt = 1.0 / cnt.astype(jnp.float32)
        pooled = acc * inv_cnt - pext_ref[HALO:HALO + tm, cols]
        pooled_ref[:, cols] = pooled.astype(jnp.bfloat16)
    for g in range(len(POOL_WINDOWS)):
        cols = slice(g * POOL_GROUP_W, (g + 1) * POOL_GROUP_W)
        m_ref[:, cols] = (jnp.dot(pooled_ref[:, cols], w_pool_ref[g], preferred_element_type=jnp.float32)
                          + b_pool_ref[:, cols])
    br_b = m_ref[...] * pool_scale_ref[...] * _silu(in_proj(xb, COL_ZB))
    proj_b = jnp.dot(br_b.astype(jnp.bfloat16), w_br_b_ref[...], preferred_element_type=jnp.float32)
    merged = merged + jax.nn.sigmoid(in_proj(xb, COL_GB)) * proj_b

    out = jnp.dot(merged.astype(jnp.bfloat16), w_out_ref[...], preferred_element_type=jnp.float32) + b_out_ref[...]
    out_ref[...] = _layernorm(DEEPNORM_ALPHA * x_ref[...] + out, ln_g_ref[...], ln_b_ref[...])


def _resident(shape):
    return pl.BlockSpec(shape, lambda i: (0,) * len(shape), pipeline_mode=pl.Buffered(1))


def _run_layer(x, params, tm=TOKEN_TILE):
    bsz, seq_len, d = x.shape
    assert d == D_MODEL and seq_len % tm == 0 and tm % CHUNK == 0 and tm % HALO == 0
    n_tok = bsz * seq_len
    x2 = x.reshape(n_tok, d)
    halo_per_tile = tm // HALO
    n_halo_blocks = n_tok // HALO

    in_specs = [
        pl.BlockSpec((HALO, d), lambda i: (jnp.maximum(i * halo_per_tile - 1, 0), 0)),
        pl.BlockSpec((tm, d), lambda i: (i, 0)),
        pl.BlockSpec((HALO, d), lambda i: (jnp.minimum((i + 1) * halo_per_tile, n_halo_blocks - 1), 0)),
    ] + [_resident(p.shape) for p in params]

    scratch = [
        pltpu.VMEM((tm + 2 * HALO, d), jnp.bfloat16),
        pltpu.VMEM((tm, d), jnp.bfloat16),
        pltpu.VMEM((tm, d), jnp.float32),
        pltpu.VMEM((tm + 2 * HALO, d), jnp.float32),
        pltpu.VMEM((tm, d), jnp.bfloat16),
        pltpu.VMEM((tm, d), jnp.float32),
    ]
    y = pl.pallas_call(
        functools.partial(_layer_kernel, seq_len=seq_len),
        grid=(n_tok // tm,),
        in_specs=in_specs,
        out_specs=pl.BlockSpec((tm, d), lambda i: (i, 0)),
        out_shape=jax.ShapeDtypeStruct((n_tok, d), x.dtype),
        scratch_shapes=scratch,
        compiler_params=pltpu.CompilerParams(
            dimension_semantics=("arbitrary",), vmem_limit_bytes=VMEM_LIMIT_BYTES),
    )(x2, x2, x2, *params)
    return y.reshape(bsz, seq_len, d)


def kernel(x_prompt, x_sample, w_in, b_in, ln_v_g, ln_v_b, w_spatial, b_spatial, w_pool, b_pool, pool_scale,
           w_br_a, w_br_b, w_out, b_out, ln_g, ln_b):
    assert w_in.shape[0] == DEPTH
    bf16 = jnp.bfloat16
    row = lambda a: a[0].reshape(1, -1)
    b_sp = jnp.repeat(b_spatial[0].T, A_GROUP_W, axis=1)
    params = (
        w_in[0].astype(bf16), row(b_in), row(ln_v_g), row(ln_v_b), w_spatial[0].astype(bf16), b_sp,
        w_pool[0].astype(bf16), row(b_pool), row(pool_scale), w_br_a[0].astype(bf16), w_br_b[0].astype(bf16),
        w_out[0].astype(bf16), row(b_out), row(ln_g), row(ln_b),
    )
    return _run_layer(x_prompt, params), _run_layer(x_sample, params)
```

```python
import functools

import numpy as np
import jax
import jax.numpy as jnp
from jax.experimental import pallas as pl
from jax.experimental.pallas import tpu as pltpu

D_MODEL = 1024
CHUNK = 128
A_GROUPS = 8
A_GROUP_W = D_MODEL // A_GROUPS
POOL_WINDOWS = (2, 4, 8, 16)
POOL_GROUP_W = D_MODEL // len(POOL_WINDOWS)
DEPTH = 1
DEEPNORM_ALPHA = (2.0 * DEPTH) ** 0.25
LN_EPS = 1e-5

COL_U, COL_V, COL_ZA, COL_P, COL_ZB, COL_GA, COL_GB = (i * D_MODEL for i in range(7))

HALO = 16
BAND_K = CHUNK + 2 * HALO
TOKEN_TILE = 512
SUB_TILE = 256
SCRATCH_SLOTS = 2
VMEM_LIMIT_BYTES = 58 * 1024 * 1024


def _layernorm(x, g, b):
    mu = jnp.mean(x, axis=-1, keepdims=True)
    xc = x - mu
    var = jnp.mean(xc * xc, axis=-1, keepdims=True)
    return xc * jax.lax.rsqrt(var + LN_EPS) * g + b


def _silu(z):
    return z * jax.nn.sigmoid(z)


def _layer_kernel(xprev_ref, x_ref, xnext_ref, w_in_ref, b_in_ref, ln_v_g_ref, ln_v_b_ref, w_sp_ref,
                  b_sp_ref, w_pool_ref, b_pool_ref, pool_scale_ref, w_br_a_ref, w_br_b_ref, w_out_ref,
                  b_out_ref, ln_g_ref, ln_b_ref, band_ref, out_ref,
                  xext_ref, vn_ref, vs_ref, pext_ref, pooled_ref, m_ref, *, seq_len):
    tm = x_ref.shape[0]
    sub = vn_ref.shape[1]
    n_sub = tm // sub
    n_chunks = sub // CHUNK
    tiles_per_seq = seq_len // tm
    j = pl.program_id(0) % tiles_per_seq
    is_first = j == 0
    is_last = j == tiles_per_seq - 1

    def in_proj(lhs, col):
        return (jnp.dot(lhs, w_in_ref[:, col:col + D_MODEL], preferred_element_type=jnp.float32)
                + b_in_ref[:, col:col + D_MODEL])

    xext_ref[0:HALO, :] = xprev_ref[...].astype(jnp.bfloat16)
    xext_ref[HALO:HALO + tm, :] = x_ref[...].astype(jnp.bfloat16)
    xext_ref[HALO + tm:, :] = xnext_ref[...].astype(jnp.bfloat16)

    def sub_tile(s):
        r0 = s * sub
        slot = s % vn_ref.shape[0]
        xb = xext_ref[HALO + r0:HALO + r0 + sub, :]

        v = in_proj(xb, COL_V)
        u = in_proj(xb, COL_U)
        z_a = in_proj(xb, COL_ZA)
        vn_ref[slot] = _layernorm(v, ln_v_g_ref[...], ln_v_b_ref[...]).astype(jnp.bfloat16)
        yield

        for n in range(0, n_chunks, 2):
            rows0 = slice(n * CHUNK, (n + 1) * CHUNK)
            rows1 = slice((n + 1) * CHUNK, (n + 2) * CHUNK)
            for g in range(A_GROUPS):
                cols = slice(g * A_GROUP_W, (g + 1) * A_GROUP_W)
                rhs = jnp.concatenate([vn_ref[slot, rows0, cols], vn_ref[slot, rows1, cols]], axis=1)
                mixed = jnp.dot(w_sp_ref[g], rhs, preferred_element_type=jnp.float32)
                vs_ref[slot, rows0, cols] = mixed[:, :A_GROUP_W] + b_sp_ref[:, cols]
                vs_ref[slot, rows1, cols] = mixed[:, A_GROUP_W:] + b_sp_ref[:, cols]
        g_a = in_proj(xb, COL_GA)
        p_rows = slice(0 if s == 0 else r0 + 2 * HALO, r0 + sub + 2 * HALO)
        pext_ref[p_rows, :] = in_proj(xext_ref[p_rows, :], COL_P).astype(jnp.bfloat16)
        br_a = u * vs_ref[slot] * _silu(z_a)
        proj_a = jnp.dot(br_a.astype(jnp.bfloat16), w_br_a_ref[...], preferred_element_type=jnp.float32)

        for c in range(n_chunks):
            variant = jnp.where(jnp.logical_and(is_first, s == 0 and c == 0), 0,
                                jnp.where(jnp.logical_and(is_last, s == n_sub - 1 and c == n_chunks - 1), 2, 1))
            for g in range(len(POOL_WINDOWS)):
                cols = slice(g * POOL_GROUP_W, (g + 1) * POOL_GROUP_W)
                pooled = jnp.dot(band_ref[variant, g],
                                 pext_ref[r0 + c * CHUNK:r0 + c * CHUNK + BAND_K, cols],
                                 preferred_element_type=jnp.float32)
                pooled_ref[slot, c * CHUNK:(c + 1) * CHUNK, cols] = pooled.astype(jnp.bfloat16)
        z_b = in_proj(xb, COL_ZB)
        for g in range(len(POOL_WINDOWS)):
            cols = slice(g * POOL_GROUP_W, (g + 1) * POOL_GROUP_W)
            m_ref[slot, :, cols] = (jnp.dot(pooled_ref[slot, :, cols], w_pool_ref[g],
                                            preferred_element_type=jnp.float32) + b_pool_ref[:, cols])
        g_b = in_proj(xb, COL_GB)
        merged = jax.nn.sigmoid(g_a) * proj_a
        br_b = m_ref[slot] * pool_scale_ref[...] * _silu(z_b)
        proj_b = jnp.dot(br_b.astype(jnp.bfloat16), w_br_b_ref[...], preferred_element_type=jnp.float32)
        merged = merged + jax.nn.sigmoid(g_b) * proj_b
        yield

        out = (jnp.dot(merged.astype(jnp.bfloat16), w_out_ref[...], preferred_element_type=jnp.float32)
               + b_out_ref[...])
        out_ref[r0:r0 + sub, :] = _layernorm(DEEPNORM_ALPHA * x_ref[r0:r0 + sub, :] + out,
                                             ln_g_ref[...], ln_b_ref[...])
        yield

    tiles = [sub_tile(s) for s in range(n_sub)]
    next(tiles[0])
    for s in range(n_sub):
        next(tiles[s])
        if s + 1 < n_sub:
            next(tiles[s + 1])
        next(tiles[s])


def _pool_band_table():
    t = np.arange(CHUNK)[:, None]
    k = np.arange(BAND_K)[None, :] - HALO
    table = np.zeros((3, len(POOL_WINDOWS), CHUNK, BAND_K), np.float32)
    for variant, valid in enumerate((k >= 0, np.ones_like(k, bool), k < CHUNK)):
        for g, w in enumerate(POOL_WINDOWS):
            in_window = (k >= t - w // 2) & (k < t + w - w // 2) & valid
            table[variant, g] = in_window / in_window.sum(axis=1, keepdims=True) - (k == t)
    return jnp.asarray(table, jnp.bfloat16)


def _resident(shape):
    return pl.BlockSpec(shape, lambda i: (0,) * len(shape), pipeline_mode=pl.Buffered(1))


def _run_layer(x, params, tm=TOKEN_TILE, sub=SUB_TILE):
    bsz, seq_len, d = x.shape
    assert d == D_MODEL and seq_len % tm == 0 and tm % sub == 0 and sub % (2 * CHUNK) == 0
    assert seq_len >= 2 * CHUNK
    n_tok = bsz * seq_len
    x2 = x.reshape(n_tok, d)
    halo_per_tile = tm // HALO
    n_halo_blocks = n_tok // HALO

    in_specs = [
        pl.BlockSpec((HALO, d), lambda i: (jnp.maximum(i * halo_per_tile - 1, 0), 0)),
        pl.BlockSpec((tm, d), lambda i: (i, 0)),
        pl.BlockSpec((HALO, d), lambda i: (jnp.minimum((i + 1) * halo_per_tile, n_halo_blocks - 1), 0)),
    ] + [_resident(p.shape) for p in params]

    scratch = [
        pltpu.VMEM((tm + 2 * HALO, d), jnp.bfloat16),
        pltpu.VMEM((SCRATCH_SLOTS, sub, d), jnp.bfloat16),
        pltpu.VMEM((SCRATCH_SLOTS, sub, d), jnp.float32),
        pltpu.VMEM((tm + 2 * HALO, d), jnp.bfloat16),
        pltpu.VMEM((SCRATCH_SLOTS, sub, d), jnp.bfloat16),
        pltpu.VMEM((SCRATCH_SLOTS, sub, d), jnp.float32),
    ]
    y = pl.pallas_call(
        functools.partial(_layer_kernel, seq_len=seq_len),
        grid=(n_tok // tm,),
        in_specs=in_specs,
        out_specs=pl.BlockSpec((tm, d), lambda i: (i, 0)),
        out_shape=jax.ShapeDtypeStruct((n_tok, d), x.dtype),
        scratch_shapes=scratch,
        compiler_params=pltpu.CompilerParams(
            dimension_semantics=("arbitrary",), vmem_limit_bytes=VMEM_LIMIT_BYTES),
    )(x2, x2, x2, *params)
    return y.reshape(bsz, seq_len, d)


def kernel(x_prompt, x_sample, w_in, b_in, ln_v_g, ln_v_b, w_spatial, b_spatial, w_pool, b_pool, pool_scale,
           w_br_a, w_br_b, w_out, b_out, ln_g, ln_b):
    assert w_in.shape[0] == DEPTH
    bf16 = jnp.bfloat16
    row = lambda a: a[0].reshape(1, -1)
    b_sp = jnp.repeat(b_spatial[0].T, A_GROUP_W, axis=1)
    params = (
        w_in[0].astype(bf16), row(b_in), row(ln_v_g), row(ln_v_b), w_spatial[0].astype(bf16), b_sp,
        w_pool[0].astype(bf16), row(b_pool), row(pool_scale), w_br_a[0].astype(bf16), w_br_b[0].astype(bf16),
        w_out[0].astype(bf16), row(b_out), row(ln_g), row(ln_b), _pool_band_table(),
    )
    return _run_layer(x_prompt, params), _run_layer(x_sample, params)
```

```python
import functools

import numpy as np
import jax
import jax.numpy as jnp
from jax.experimental import pallas as pl
from jax.experimental.pallas import tpu as pltpu

D_MODEL = 1024
CHUNK = 128
A_GROUPS = 8
A_GROUP_W = D_MODEL // A_GROUPS
POOL_WINDOWS = (2, 4, 8, 16)
POOL_GROUP_W = D_MODEL // len(POOL_WINDOWS)
DEPTH = 1
DEEPNORM_ALPHA = (2.0 * DEPTH) ** 0.25
LN_EPS = 1e-5

COL_U, COL_V, COL_ZA, COL_P, COL_ZB, COL_GA, COL_GB = (i * D_MODEL for i in range(7))

HALO = 16
BAND_K = CHUNK + 2 * HALO
TOKEN_TILE = 512
SUB_TILE = 256
SCRATCH_SLOTS = 2
VMEM_LIMIT_BYTES = 58 * 1024 * 1024


def _layernorm(x, g, b):
    mu = jnp.mean(x, axis=-1, keepdims=True)
    xc = x - mu
    var = jnp.mean(xc * xc, axis=-1, keepdims=True)
    return xc * jax.lax.rsqrt(var + LN_EPS) * g + b


def _silu(z):
    return z * jax.nn.sigmoid(z)


def _pack_bf16(w):
    k, n = w.shape[-2:]
    pairs = w.astype(jnp.bfloat16).reshape(*w.shape[:-2], k // 2, 2, n)
    return jax.lax.bitcast_convert_type(jnp.swapaxes(pairs, -1, -2), jnp.uint32)


def _as_bf16(packed):
    return pltpu.bitcast(packed, jnp.bfloat16)


def _layer_kernel(xprev_ref, x_ref, xnext_ref, w_in_ref, b_in_ref, ln_v_g_ref, ln_v_b_ref, w_sp_ref,
                  b_sp_ref, w_pool_ref, b_pool_ref, pool_scale_ref, w_br_a_ref, w_br_b_ref, w_out_ref,
                  b_out_ref, ln_g_ref, ln_b_ref, band_ref, out_ref,
                  xext_ref, vn_ref, vs_ref, pext_ref, m_ref, wpp_ref, bpp_ref, *, seq_len):
    tm = x_ref.shape[0]
    sub = vn_ref.shape[1]
    n_sub = tm // sub
    n_chunks = sub // CHUNK
    tiles_per_seq = seq_len // tm
    j = pl.program_id(0) % tiles_per_seq
    is_first = j == 0
    is_last = j == tiles_per_seq - 1

    def in_proj(lhs, col):
        return (jnp.dot(lhs, _as_bf16(w_in_ref[:, col:col + D_MODEL]), preferred_element_type=jnp.float32)
                + b_in_ref[:, col:col + D_MODEL])

    @pl.when(pl.program_id(0) == 0)
    def _():
        for g in range(len(POOL_WINDOWS)):
            cols = slice(g * POOL_GROUP_W, (g + 1) * POOL_GROUP_W)
            pcols = slice(COL_P + g * POOL_GROUP_W, COL_P + (g + 1) * POOL_GROUP_W)
            w_pool_g = _as_bf16(w_pool_ref[g])
            wpp_ref[:, cols] = jnp.dot(_as_bf16(w_in_ref[:, pcols]), w_pool_g,
                                       preferred_element_type=jnp.float32).astype(jnp.bfloat16)
            b_p = jnp.broadcast_to(b_in_ref[:, pcols], bpp_ref[:, cols].shape).astype(jnp.bfloat16)
            bpp_ref[:, cols] = jnp.dot(b_p, w_pool_g, preferred_element_type=jnp.float32)

    xext_ref[0:HALO, :] = xprev_ref[...].astype(jnp.bfloat16)
    xext_ref[HALO:HALO + tm, :] = x_ref[...].astype(jnp.bfloat16)
    xext_ref[HALO + tm:, :] = xnext_ref[...].astype(jnp.bfloat16)

    def sub_tile(s):
        r0 = s * sub
        slot = s % vn_ref.shape[0]
        xb = xext_ref[HALO + r0:HALO + r0 + sub, :]

        v = in_proj(xb, COL_V)
        u = in_proj(xb, COL_U)
        z_a = in_proj(xb, COL_ZA)
        vn_ref[slot] = _layernorm(v, ln_v_g_ref[...], ln_v_b_ref[...]).astype(jnp.bfloat16)
        yield

        for n in range(0, n_chunks, 2):
            rows0 = slice(n * CHUNK, (n + 1) * CHUNK)
            rows1 = slice((n + 1) * CHUNK, (n + 2) * CHUNK)
            for g in range(A_GROUPS):
                cols = slice(g * A_GROUP_W, (g + 1) * A_GROUP_W)
                rhs = jnp.concatenate([vn_ref[slot, rows0, cols], vn_ref[slot, rows1, cols]], axis=1)
                mixed = jnp.dot(_as_bf16(w_sp_ref[g]), rhs, preferred_element_type=jnp.float32)
                vs_ref[slot, rows0, cols] = mixed[:, :A_GROUP_W] + b_sp_ref[:, cols]
                vs_ref[slot, rows1, cols] = mixed[:, A_GROUP_W:] + b_sp_ref[:, cols]
        g_a = in_proj(xb, COL_GA)
        p_rows = slice(0 if s == 0 else r0 + 2 * HALO, r0 + sub + 2 * HALO)
        pext_ref[p_rows, :] = (jnp.dot(xext_ref[p_rows, :], wpp_ref[...], preferred_element_type=jnp.float32)
                               + bpp_ref[0:1, :]).astype(jnp.bfloat16)
        br_a = u * vs_ref[slot] * _silu(z_a)
        proj_a = jnp.dot(br_a.astype(jnp.bfloat16), _as_bf16(w_br_a_ref[...]), preferred_element_type=jnp.float32)

        for c in range(n_chunks):
            variant = jnp.where(jnp.logical_and(is_first, s == 0 and c == 0), 0,
                                jnp.where(jnp.logical_and(is_last, s == n_sub - 1 and c == n_chunks - 1), 2, 1))
            for g in range(len(POOL_WINDOWS)):
                cols = slice(g * POOL_GROUP_W, (g + 1) * POOL_GROUP_W)
                m_ref[slot, c * CHUNK:(c + 1) * CHUNK, cols] = (
                    jnp.dot(_as_bf16(band_ref[variant, g]), pext_ref[r0 + c * CHUNK:r0 + c * CHUNK + BAND_K, cols],
                            preferred_element_type=jnp.float32) + b_pool_ref[:, cols])
        z_b = in_proj(xb, COL_ZB)
        g_b = in_proj(xb, COL_GB)
        merged = jax.nn.sigmoid(g_a) * proj_a
        br_b = m_ref[slot] * pool_scale_ref[...] * _silu(z_b)
        proj_b = jnp.dot(br_b.astype(jnp.bfloat16), _as_bf16(w_br_b_ref[...]), preferred_element_type=jnp.float32)
        merged = merged + jax.nn.sigmoid(g_b) * proj_b
        yield

        out = (jnp.dot(merged.astype(jnp.bfloat16), _as_bf16(w_out_ref[...]), preferred_element_type=jnp.float32)
               + b_out_ref[...])
        out_ref[r0:r0 + sub, :] = _layernorm(DEEPNORM_ALPHA * x_ref[r0:r0 + sub, :] + out,
                                             ln_g_ref[...], ln_b_ref[...])
        yield

    tiles = [sub_tile(s) for s in range(n_sub)]
    next(tiles[0])
    for s in range(n_sub):
        next(tiles[s])
        if s + 1 < n_sub:
            next(tiles[s + 1])
        next(tiles[s])


def _pool_band_table():
    t = np.arange(CHUNK)[:, None]
    k = np.arange(BAND_K)[None, :] - HALO
    table = np.zeros((3, len(POOL_WINDOWS), CHUNK, BAND_K), np.float32)
    for variant, valid in enumerate((k >= 0, np.ones_like(k, bool), k < CHUNK)):
        for g, w in enumerate(POOL_WINDOWS):
            in_window = (k >= t - w // 2) & (k < t + w - w // 2) & valid
            table[variant, g] = in_window / in_window.sum(axis=1, keepdims=True) - (k == t)
    return jnp.asarray(table)


def _resident(shape):
    return pl.BlockSpec(shape, lambda i: (0,) * len(shape), pipeline_mode=pl.Buffered(1))


def _run_layer(x, params, tm=TOKEN_TILE, sub=SUB_TILE):
    bsz, seq_len, d = x.shape
    assert d == D_MODEL and seq_len % tm == 0 and tm % sub == 0 and sub % (2 * CHUNK) == 0
    assert seq_len >= 2 * CHUNK
    n_tok = bsz * seq_len
    x2 = x.reshape(n_tok, d)
    halo_per_tile = tm // HALO
    n_halo_blocks = n_tok // HALO

    in_specs = [
        pl.BlockSpec((HALO, d), lambda i: (jnp.maximum(i * halo_per_tile - 1, 0), 0)),
        pl.BlockSpec((tm, d), lambda i: (i, 0)),
        pl.BlockSpec((HALO, d), lambda i: (jnp.minimum((i + 1) * halo_per_tile, n_halo_blocks - 1), 0)),
    ] + [_resident(p.shape) for p in params]

    scratch = [
        pltpu.VMEM((tm + 2 * HALO, d), jnp.bfloat16),
        pltpu.VMEM((SCRATCH_SLOTS, sub, d), jnp.bfloat16),
        pltpu.VMEM((SCRATCH_SLOTS, sub, d), jnp.float32),
        pltpu.VMEM((tm + 2 * HALO, d), jnp.bfloat16),
        pltpu.VMEM((SCRATCH_SLOTS, sub, d), jnp.float32),
        pltpu.VMEM((d, d), jnp.bfloat16),
        pltpu.VMEM((HALO, d), jnp.float32),
    ]
    y = pl.pallas_call(
        functools.partial(_layer_kernel, seq_len=seq_len),
        grid=(n_tok // tm,),
        in_specs=in_specs,
        out_specs=pl.BlockSpec((tm, d), lambda i: (i, 0)),
        out_shape=jax.ShapeDtypeStruct((n_tok, d), x.dtype),
        scratch_shapes=scratch,
        compiler_params=pltpu.CompilerParams(
            dimension_semantics=("arbitrary",), vmem_limit_bytes=VMEM_LIMIT_BYTES),
    )(x2, x2, x2, *params)
    return y.reshape(bsz, seq_len, d)


def kernel(x_prompt, x_sample, w_in, b_in, ln_v_g, ln_v_b, w_spatial, b_spatial, w_pool, b_pool, pool_scale,
           w_br_a, w_br_b, w_out, b_out, ln_g, ln_b):
    assert w_in.shape[0] == DEPTH
    row = lambda a: a[0].reshape(1, -1)
    b_sp = jnp.repeat(b_spatial[0].T, A_GROUP_W, axis=1)
    params = (
        _pack_bf16(w_in[0]), row(b_in), row(ln_v_g), row(ln_v_b), _pack_bf16(w_spatial[0]), b_sp,
        _pack_bf16(w_pool[0]), row(b_pool), row(pool_scale), _pack_bf16(w_br_a[0]), _pack_bf16(w_br_b[0]),
        _pack_bf16(w_out[0]), row(b_out), row(ln_g), row(ln_b), _pack_bf16(_pool_band_table()),
    )
    return _run_layer(x_prompt, params), _run_layer(x_sample, params)
```

```python
import functools

import numpy as np
import jax
import jax.numpy as jnp
from jax.experimental import pallas as pl
from jax.experimental.pallas import tpu as pltpu

D_MODEL = 1024
CHUNK = 128
A_GROUPS = 8
A_GROUP_W = D_MODEL // A_GROUPS
POOL_WINDOWS = (2, 4, 8, 16)
POOL_GROUP_W = D_MODEL // len(POOL_WINDOWS)
DEPTH = 1
DEEPNORM_ALPHA = (2.0 * DEPTH) ** 0.25
LN_EPS = 1e-5

COL_U, COL_V, COL_ZA, COL_P, COL_ZB, COL_GA, COL_GB = (i * D_MODEL for i in range(7))

HALO = 16
BAND_K = CHUNK + 2 * HALO
TOKEN_TILE = 512
SUB_TILE = 256
SCRATCH_SLOTS = 2
PACK_BLOCK_ROWS = 256
VMEM_LIMIT_BYTES = 58 * 1024 * 1024


def _layernorm(x, g, b):
    mu = jnp.mean(x, axis=-1, keepdims=True)
    xc = x - mu
    var = jnp.mean(xc * xc, axis=-1, keepdims=True)
    return xc * jax.lax.rsqrt(var + LN_EPS) * g + b


def _silu(z):
    return z * jax.nn.sigmoid(z)


def _pack_kernel(*refs):
    n = len(refs) // 2
    for w_ref, o_ref in zip(refs[:n], refs[n:]):
        o_ref[...] = pltpu.bitcast(w_ref[...].astype(jnp.bfloat16), jnp.uint32)


def _pack_bf16(weights):
    rows = weights[0].shape[0]
    assert all(w.ndim == 2 and w.shape[0] == rows for w in weights) and rows % PACK_BLOCK_ROWS == 0
    return pl.pallas_call(
        _pack_kernel,
        grid=(rows // PACK_BLOCK_ROWS,),
        in_specs=[pl.BlockSpec((PACK_BLOCK_ROWS, w.shape[1]), lambda i: (i, 0)) for w in weights],
        out_specs=[pl.BlockSpec((PACK_BLOCK_ROWS // 2, w.shape[1]), lambda i: (i, 0)) for w in weights],
        out_shape=[jax.ShapeDtypeStruct((rows // 2, w.shape[1]), jnp.uint32) for w in weights],
        compiler_params=pltpu.CompilerParams(
            dimension_semantics=("arbitrary",), vmem_limit_bytes=VMEM_LIMIT_BYTES),
    )(*weights)


def _as_bf16(packed):
    return pltpu.bitcast(packed, jnp.bfloat16)


def _layer_kernel(xprev_ref, x_ref, xnext_ref, w_in_ref, b_in_ref, ln_v_g_ref, ln_v_b_ref, w_sp_ref,
                  b_sp_ref, w_pool_ref, b_pool_ref, pool_scale_ref, w_br_a_ref, w_br_b_ref, w_out_ref,
                  b_out_ref, ln_g_ref, ln_b_ref, band_ref, out_ref,
                  xext_ref, vn_ref, vs_ref, pext_ref, m_ref, wpp_ref, bpp_ref, *, seq_len):
    tm = x_ref.shape[0]
    sub = vn_ref.shape[1]
    n_sub = tm // sub
    n_chunks = sub // CHUNK
    tiles_per_seq = seq_len // tm
    j = pl.program_id(0) % tiles_per_seq
    is_first = j == 0
    is_last = j == tiles_per_seq - 1

    def in_proj(lhs, col):
        return (jnp.dot(lhs, _as_bf16(w_in_ref[:, col:col + D_MODEL]), preferred_element_type=jnp.float32)
                + b_in_ref[:, col:col + D_MODEL])

    @pl.when(pl.program_id(0) == 0)
    def _():
        for g in range(len(POOL_WINDOWS)):
            cols = slice(g * POOL_GROUP_W, (g + 1) * POOL_GROUP_W)
            pcols = slice(COL_P + g * POOL_GROUP_W, COL_P + (g + 1) * POOL_GROUP_W)
            w_pool_g = _as_bf16(w_pool_ref[g])
            wpp_ref[:, cols] = jnp.dot(_as_bf16(w_in_ref[:, pcols]), w_pool_g,
                                       preferred_element_type=jnp.float32).astype(jnp.bfloat16)
            b_p = jnp.broadcast_to(b_in_ref[:, pcols], bpp_ref[:, cols].shape).astype(jnp.bfloat16)
            bpp_ref[:, cols] = jnp.dot(b_p, w_pool_g, preferred_element_type=jnp.float32)

    xext_ref[0:HALO, :] = xprev_ref[...].astype(jnp.bfloat16)
    xext_ref[HALO:HALO + tm, :] = x_ref[...].astype(jnp.bfloat16)
    xext_ref[HALO + tm:, :] = xnext_ref[...].astype(jnp.bfloat16)

    def sub_tile(s):
        r0 = s * sub
        slot = s % vn_ref.shape[0]
        xb = xext_ref[HALO + r0:HALO + r0 + sub, :]

        v = in_proj(xb, COL_V)
        u = in_proj(xb, COL_U)
        z_a = in_proj(xb, COL_ZA)
        vn_ref[slot] = _layernorm(v, ln_v_g_ref[...], ln_v_b_ref[...]).astype(jnp.bfloat16)
        yield

        for n in range(0, n_chunks, 2):
            rows0 = slice(n * CHUNK, (n + 1) * CHUNK)
            rows1 = slice((n + 1) * CHUNK, (n + 2) * CHUNK)
            for g in range(A_GROUPS):
                cols = slice(g * A_GROUP_W, (g + 1) * A_GROUP_W)
                rhs = jnp.concatenate([vn_ref[slot, rows0, cols], vn_ref[slot, rows1, cols]], axis=1)
                mixed = jnp.dot(_as_bf16(w_sp_ref[g]), rhs, preferred_element_type=jnp.float32)
                vs_ref[slot, rows0, cols] = mixed[:, :A_GROUP_W] + b_sp_ref[:, cols]
                vs_ref[slot, rows1, cols] = mixed[:, A_GROUP_W:] + b_sp_ref[:, cols]
        g_a = in_proj(xb, COL_GA)
        p_rows = slice(0 if s == 0 else r0 + 2 * HALO, r0 + sub + 2 * HALO)
        pext_ref[p_rows, :] = (jnp.dot(xext_ref[p_rows, :], wpp_ref[...], preferred_element_type=jnp.float32)
                               + bpp_ref[0:1, :]).astype(jnp.bfloat16)
        br_a = u * vs_ref[slot] * _silu(z_a)
        proj_a = jnp.dot(br_a.astype(jnp.bfloat16), _as_bf16(w_br_a_ref[...]), preferred_element_type=jnp.float32)

        for c in range(n_chunks):
            variant = jnp.where(jnp.logical_and(is_first, s == 0 and c == 0), 0,
                                jnp.where(jnp.logical_and(is_last, s == n_sub - 1 and c == n_chunks - 1), 2, 1))
            for g in range(len(POOL_WINDOWS)):
                cols = slice(g * POOL_GROUP_W, (g + 1) * POOL_GROUP_W)
                m_ref[slot, c * CHUNK:(c + 1) * CHUNK, cols] = (
                    jnp.dot(_as_bf16(band_ref[variant, g]), pext_ref[r0 + c * CHUNK:r0 + c * CHUNK + BAND_K, cols],
                            preferred_element_type=jnp.float32) + b_pool_ref[:, cols])
        z_b = in_proj(xb, COL_ZB)
        g_b = in_proj(xb, COL_GB)
        merged = jax.nn.sigmoid(g_a) * proj_a
        br_b = m_ref[slot] * pool_scale_ref[...] * _silu(z_b)
        proj_b = jnp.dot(br_b.astype(jnp.bfloat16), _as_bf16(w_br_b_ref[...]), preferred_element_type=jnp.float32)
        merged = merged + jax.nn.sigmoid(g_b) * proj_b
        yield

        out = (jnp.dot(merged.astype(jnp.bfloat16), _as_bf16(w_out_ref[...]), preferred_element_type=jnp.float32)
               + b_out_ref[...])
        out_ref[r0:r0 + sub, :] = _layernorm(DEEPNORM_ALPHA * x_ref[r0:r0 + sub, :] + out,
                                             ln_g_ref[...], ln_b_ref[...])
        yield

    tiles = [sub_tile(s) for s in range(n_sub)]
    next(tiles[0])
    for s in range(n_sub):
        next(tiles[s])
        if s + 1 < n_sub:
            next(tiles[s + 1])
        next(tiles[s])


def _pool_band_table():
    t = np.arange(CHUNK)[:, None]
    k = np.arange(BAND_K)[None, :] - HALO
    table = np.zeros((3, len(POOL_WINDOWS), CHUNK, BAND_K), np.float32)
    for variant, valid in enumerate((k >= 0, np.ones_like(k, bool), k < CHUNK)):
        for g, w in enumerate(POOL_WINDOWS):
            in_window = (k >= t - w // 2) & (k < t + w - w // 2) & valid
            table[variant, g] = in_window / in_window.sum(axis=1, keepdims=True) - (k == t)
    bits = table.astype(jnp.bfloat16).view(np.uint16).astype(np.uint32)
    return jnp.asarray(bits[..., 0::2, :] | (bits[..., 1::2, :] << 16))


def _resident(shape):
    return pl.BlockSpec(shape, lambda i: (0,) * len(shape), pipeline_mode=pl.Buffered(1))


def _run_layer(x, params, tm=TOKEN_TILE, sub=SUB_TILE):
    bsz, seq_len, d = x.shape
    assert d == D_MODEL and seq_len % tm == 0 and tm % sub == 0 and sub % (2 * CHUNK) == 0
    assert seq_len >= 2 * CHUNK
    n_tok = bsz * seq_len
    x2 = x.reshape(n_tok, d)
    halo_per_tile = tm // HALO
    n_halo_blocks = n_tok // HALO

    in_specs = [
        pl.BlockSpec((HALO, d), lambda i: (jnp.maximum(i * halo_per_tile - 1, 0), 0)),
        pl.BlockSpec((tm, d), lambda i: (i, 0)),
        pl.BlockSpec((HALO, d), lambda i: (jnp.minimum((i + 1) * halo_per_tile, n_halo_blocks - 1), 0)),
    ] + [_resident(p.shape) for p in params]

    scratch = [
        pltpu.VMEM((tm + 2 * HALO, d), jnp.bfloat16),
        pltpu.VMEM((SCRATCH_SLOTS, sub, d), jnp.bfloat16),
        pltpu.VMEM((SCRATCH_SLOTS, sub, d), jnp.float32),
        pltpu.VMEM((tm + 2 * HALO, d), jnp.bfloat16),
        pltpu.VMEM((SCRATCH_SLOTS, sub, d), jnp.float32),
        pltpu.VMEM((d, d), jnp.bfloat16),
        pltpu.VMEM((HALO, d), jnp.float32),
    ]
    y = pl.pallas_call(
        functools.partial(_layer_kernel, seq_len=seq_len),
        grid=(n_tok // tm,),
        in_specs=in_specs,
        out_specs=pl.BlockSpec((tm, d), lambda i: (i, 0)),
        out_shape=jax.ShapeDtypeStruct((n_tok, d), x.dtype),
        scratch_shapes=scratch,
        compiler_params=pltpu.CompilerParams(
            dimension_semantics=("arbitrary",), vmem_limit_bytes=VMEM_LIMIT_BYTES),
    )(x2, x2, x2, *params)
    return y.reshape(bsz, seq_len, d)


def kernel(x_prompt, x_sample, w_in, b_in, ln_v_g, ln_v_b, w_spatial, b_spatial, w_pool, b_pool, pool_scale,
           w_br_a, w_br_b, w_out, b_out, ln_g, ln_b):
    assert w_in.shape[0] == DEPTH
    row = lambda a: a[0].reshape(1, -1)
    b_sp = jnp.repeat(b_spatial[0].T, A_GROUP_W, axis=1)
    w_in_p, w_sp_p, w_pool_p, w_br_a_p, w_br_b_p, w_out_p = _pack_bf16([
        w_in[0], w_spatial[0].reshape(A_GROUPS * CHUNK, CHUNK),
        w_pool[0].reshape(len(POOL_WINDOWS) * POOL_GROUP_W, POOL_GROUP_W), w_br_a[0], w_br_b[0], w_out[0]])
    params = (
        w_in_p, row(b_in), row(ln_v_g), row(ln_v_b), w_sp_p.reshape(A_GROUPS, CHUNK // 2, CHUNK), b_sp,
        w_pool_p.reshape(len(POOL_WINDOWS), POOL_GROUP_W // 2, POOL_GROUP_W), row(b_pool), row(pool_scale),
        w_br_a_p, w_br_b_p, w_out_p, row(b_out), row(ln_g), row(ln_b), _pool_band_table(),
    )
    return _run_layer(x_prompt, params), _run_layer(x_sample, params)
```

```python
import functools

import numpy as np
import jax
import jax.numpy as jnp
from jax.experimental import pallas as pl
from jax.experimental.pallas import tpu as pltpu

D_MODEL = 1024
CHUNK = 128
A_GROUPS = 8
A_GROUP_W = D_MODEL // A_GROUPS
POOL_WINDOWS = (2, 4, 8, 16)
POOL_GROUP_W = D_MODEL // len(POOL_WINDOWS)
DEPTH = 1
DEEPNORM_ALPHA = (2.0 * DEPTH) ** 0.25
LN_EPS = 1e-5

COL_U, COL_V, COL_ZA, COL_P, COL_ZB, COL_GA, COL_GB = (i * D_MODEL for i in range(7))

HALO = 16
BAND_K = CHUNK + 2 * HALO
TOKEN_TILE = 1024
SUB_TILE = 256
SCRATCH_SLOTS = 2
PACK_BLOCK_ROWS = 256
VMEM_LIMIT_BYTES = 58 * 1024 * 1024


def _layernorm(x, g, b):
    mu = jnp.mean(x, axis=-1, keepdims=True)
    xc = x - mu
    var = jnp.mean(xc * xc, axis=-1, keepdims=True)
    return xc * jax.lax.rsqrt(var + LN_EPS) * g + b


def _silu(z):
    return z * jax.nn.sigmoid(z)


def _pack_kernel(*refs):
    n = len(refs) // 2
    for w_ref, o_ref in zip(refs[:n], refs[n:]):
        o_ref[...] = pltpu.bitcast(w_ref[...].astype(jnp.bfloat16), jnp.uint32)


def _pack_bf16(weights):
    rows = weights[0].shape[0]
    assert all(w.ndim == 2 and w.shape[0] == rows for w in weights) and rows % PACK_BLOCK_ROWS == 0
    return pl.pallas_call(
        _pack_kernel,
        grid=(rows // PACK_BLOCK_ROWS,),
        in_specs=[pl.BlockSpec((PACK_BLOCK_ROWS, w.shape[1]), lambda i: (i, 0)) for w in weights],
        out_specs=[pl.BlockSpec((PACK_BLOCK_ROWS // 2, w.shape[1]), lambda i: (i, 0)) for w in weights],
        out_shape=[jax.ShapeDtypeStruct((rows // 2, w.shape[1]), jnp.uint32) for w in weights],
        compiler_params=pltpu.CompilerParams(
            dimension_semantics=("arbitrary",), vmem_limit_bytes=VMEM_LIMIT_BYTES),
    )(*weights)


def _as_bf16(packed):
    return pltpu.bitcast(packed, jnp.bfloat16)


def _layer_kernel(xprev_ref, x_ref, xnext_ref, w_in_ref, b_in_ref, ln_v_g_ref, ln_v_b_ref, w_sp_ref,
                  b_sp_ref, w_pool_ref, b_pool_ref, pool_scale_ref, w_br_a_ref, w_br_b_ref, w_out_ref,
                  b_out_ref, ln_g_ref, ln_b_ref, band_ref, out_ref,
                  xext_ref, vn_ref, vs_ref, pext_ref, m_ref, wpp_ref, bpp_ref, *, seq_len):
    tm = x_ref.shape[0]
    sub = vn_ref.shape[1]
    n_sub = tm // sub
    n_chunks = sub // CHUNK
    tiles_per_seq = seq_len // tm
    j = pl.program_id(0) % tiles_per_seq
    is_first = j == 0
    is_last = j == tiles_per_seq - 1

    def in_proj(lhs, col):
        return (jnp.dot(lhs, _as_bf16(w_in_ref[:, col:col + D_MODEL]), preferred_element_type=jnp.float32)
                + b_in_ref[:, col:col + D_MODEL])

    @pl.when(pl.program_id(0) == 0)
    def _():
        for g in range(len(POOL_WINDOWS)):
            cols = slice(g * POOL_GROUP_W, (g + 1) * POOL_GROUP_W)
            pcols = slice(COL_P + g * POOL_GROUP_W, COL_P + (g + 1) * POOL_GROUP_W)
            w_pool_g = _as_bf16(w_pool_ref[g])
            wpp_ref[:, cols] = jnp.dot(_as_bf16(w_in_ref[:, pcols]), w_pool_g,
                                       preferred_element_type=jnp.float32).astype(jnp.bfloat16)
            b_p = jnp.broadcast_to(b_in_ref[:, pcols], bpp_ref[:, cols].shape).astype(jnp.bfloat16)
            bpp_ref[:, cols] = jnp.dot(b_p, w_pool_g, preferred_element_type=jnp.float32)

    def sub_tile(s):
        r0 = s * sub
        slot = s % vn_ref.shape[0]
        ext_rows = slice(0 if s == 0 else r0 + 2 * HALO, r0 + sub + 2 * HALO)
        if s == 0:
            xext_ref[0:HALO, :] = xprev_ref[...].astype(jnp.bfloat16)
        own = slice(max(ext_rows.start - HALO, 0), min(ext_rows.stop - HALO, tm))
        xext_ref[HALO + own.start:HALO + own.stop, :] = x_ref[own, :].astype(jnp.bfloat16)
        if s == n_sub - 1:
            xext_ref[HALO + tm:, :] = xnext_ref[...].astype(jnp.bfloat16)
        xb = xext_ref[HALO + r0:HALO + r0 + sub, :]

        v = in_proj(xb, COL_V)
        u = in_proj(xb, COL_U)
        z_a = in_proj(xb, COL_ZA)
        vn_ref[slot] = _layernorm(v, ln_v_g_ref[...], ln_v_b_ref[...]).astype(jnp.bfloat16)
        yield

        for n in range(0, n_chunks, 2):
            rows0 = slice(n * CHUNK, (n + 1) * CHUNK)
            rows1 = slice((n + 1) * CHUNK, (n + 2) * CHUNK)
            for g in range(A_GROUPS):
                cols = slice(g * A_GROUP_W, (g + 1) * A_GROUP_W)
                rhs = jnp.concatenate([vn_ref[slot, rows0, cols], vn_ref[slot, rows1, cols]], axis=1)
                mixed = jnp.dot(_as_bf16(w_sp_ref[g]), rhs, preferred_element_type=jnp.float32)
                vs_ref[slot, rows0, cols] = mixed[:, :A_GROUP_W] + b_sp_ref[:, cols]
                vs_ref[slot, rows1, cols] = mixed[:, A_GROUP_W:] + b_sp_ref[:, cols]
        g_a = in_proj(xb, COL_GA)
        pext_ref[ext_rows, :] = (jnp.dot(xext_ref[ext_rows, :], wpp_ref[...], preferred_element_type=jnp.float32)
                                 + bpp_ref[0:1, :]).astype(jnp.bfloat16)
        br_a = u * vs_ref[slot] * _silu(z_a)
        proj_a = jnp.dot(br_a.astype(jnp.bfloat16), _as_bf16(w_br_a_ref[...]), preferred_element_type=jnp.float32)

        for c in range(n_chunks):
            variant = jnp.where(jnp.logical_and(is_first, s == 0 and c == 0), 0,
                                jnp.where(jnp.logical_and(is_last, s == n_sub - 1 and c == n_chunks - 1), 2, 1))
            for g in range(len(POOL_WINDOWS)):
                cols = slice(g * POOL_GROUP_W, (g + 1) * POOL_GROUP_W)
                m_ref[slot, c * CHUNK:(c + 1) * CHUNK, cols] = (
                    jnp.dot(_as_bf16(band_ref[variant, g]), pext_ref[r0 + c * CHUNK:r0 + c * CHUNK + BAND_K, cols],
                            preferred_element_type=jnp.float32) + b_pool_ref[:, cols])
        z_b = in_proj(xb, COL_ZB)
        g_b = in_proj(xb, COL_GB)
        merged = jax.nn.sigmoid(g_a) * proj_a
        br_b = m_ref[slot] * pool_scale_ref[...] * _silu(z_b)
        proj_b = jnp.dot(br_b.astype(jnp.bfloat16), _as_bf16(w_br_b_ref[...]), preferred_element_type=jnp.float32)
        merged = merged + jax.nn.sigmoid(g_b) * proj_b
        resid = DEEPNORM_ALPHA * x_ref[r0:r0 + sub, :] + b_out_ref[...]
        yield

        out = jnp.dot(merged.astype(jnp.bfloat16), _as_bf16(w_out_ref[...]), preferred_element_type=jnp.float32)
        out_ref[r0:r0 + sub, :] = _layernorm(resid + out, ln_g_ref[...], ln_b_ref[...])
        yield

    tiles = [sub_tile(s) for s in range(n_sub)]
    next(tiles[0])
    for s in range(n_sub):
        next(tiles[s])
        if s + 1 < n_sub:
            next(tiles[s + 1])
        next(tiles[s])


def _pool_band_table():
    t = np.arange(CHUNK)[:, None]
    k = np.arange(BAND_K)[None, :] - HALO
    table = np.zeros((3, len(POOL_WINDOWS), CHUNK, BAND_K), np.float32)
    for variant, valid in enumerate((k >= 0, np.ones_like(k, bool), k < CHUNK)):
        for g, w in enumerate(POOL_WINDOWS):
            in_window = (k >= t - w // 2) & (k < t + w - w // 2) & valid
            table[variant, g] = in_window / in_window.sum(axis=1, keepdims=True) - (k == t)
    bits = table.astype(jnp.bfloat16).view(np.uint16).astype(np.uint32)
    return jnp.asarray(bits[..., 0::2, :] | (bits[..., 1::2, :] << 16))


def _resident(shape):
    return pl.BlockSpec(shape, lambda i: (0,) * len(shape), pipeline_mode=pl.Buffered(1))


def _run_layer(x, params, tm=TOKEN_TILE, sub=SUB_TILE):
    bsz, seq_len, d = x.shape
    assert d == D_MODEL and seq_len % tm == 0 and tm % sub == 0 and sub % (2 * CHUNK) == 0
    assert seq_len >= 2 * CHUNK
    n_tok = bsz * seq_len
    x2 = x.reshape(n_tok, d)
    halo_per_tile = tm // HALO
    n_halo_blocks = n_tok // HALO

    in_specs = [
        pl.BlockSpec((HALO, d), lambda i: (jnp.maximum(i * halo_per_tile - 1, 0), 0)),
        pl.BlockSpec((tm, d), lambda i: (i, 0)),
        pl.BlockSpec((HALO, d), lambda i: (jnp.minimum((i + 1) * halo_per_tile, n_halo_blocks - 1), 0)),
    ] + [_resident(p.shape) for p in params]

    scratch = [
        pltpu.VMEM((tm + 2 * HALO, d), jnp.bfloat16),
        pltpu.VMEM((SCRATCH_SLOTS, sub, d), jnp.bfloat16),
        pltpu.VMEM((SCRATCH_SLOTS, sub, d), jnp.float32),
        pltpu.VMEM((tm + 2 * HALO, d), jnp.bfloat16),
        pltpu.VMEM((SCRATCH_SLOTS, sub, d), jnp.float32),
        pltpu.VMEM((d, d), jnp.bfloat16),
        pltpu.VMEM((HALO, d), jnp.float32),
    ]
    y = pl.pallas_call(
        functools.partial(_layer_kernel, seq_len=seq_len),
        grid=(n_tok // tm,),
        in_specs=in_specs,
        out_specs=pl.BlockSpec((tm, d), lambda i: (i, 0)),
        out_shape=jax.ShapeDtypeStruct((n_tok, d), x.dtype),
        scratch_shapes=scratch,
        compiler_params=pltpu.CompilerParams(
            dimension_semantics=("arbitrary",), vmem_limit_bytes=VMEM_LIMIT_BYTES),
    )(x2, x2, x2, *params)
    return y.reshape(bsz, seq_len, d)


def kernel(x_prompt, x_sample, w_in, b_in, ln_v_g, ln_v_b, w_spatial, b_spatial, w_pool, b_pool, pool_scale,
           w_br_a, w_br_b, w_out, b_out, ln_g, ln_b):
    assert w_in.shape[0] == DEPTH
    row = lambda a: a[0].reshape(1, -1)
    b_sp = jnp.repeat(b_spatial[0].T, A_GROUP_W, axis=1)
    w_in_p, w_sp_p, w_pool_p, w_br_a_p, w_br_b_p, w_out_p = _pack_bf16([
        w_in[0], w_spatial[0].reshape(A_GROUPS * CHUNK, CHUNK),
        w_pool[0].reshape(len(POOL_WINDOWS) * POOL_GROUP_W, POOL_GROUP_W), w_br_a[0], w_br_b[0], w_out[0]])
    params = (
        w_in_p, row(b_in), row(ln_v_g), row(ln_v_b), w_sp_p.reshape(A_GROUPS, CHUNK // 2, CHUNK), b_sp,
        w_pool_p.reshape(len(POOL_WINDOWS), POOL_GROUP_W // 2, POOL_GROUP_W), row(b_pool), row(pool_scale),
        w_br_a_p, w_br_b_p, w_out_p, row(b_out), row(ln_g), row(ln_b), _pool_band_table(),
    )
    return _run_layer(x_prompt, params), _run_layer(x_sample, params)
```

```python
import functools

import numpy as np
import jax
import jax.numpy as jnp
from jax.experimental import pallas as pl
from jax.experimental.pallas import tpu as pltpu

D_MODEL = 1024
CHUNK = 128
A_GROUPS = 8
A_GROUP_W = D_MODEL // A_GROUPS
POOL_WINDOWS = (2, 4, 8, 16)
POOL_GROUP_W = D_MODEL // len(POOL_WINDOWS)
DEPTH = 1
DEEPNORM_ALPHA = (2.0 * DEPTH) ** 0.25
LN_EPS = 1e-5

COL_U, COL_V, COL_ZA, COL_P, COL_ZB, COL_GA, COL_GB = (i * D_MODEL for i in range(7))

HALO = 16
BAND_K = CHUNK + 2 * HALO
TOKEN_TILE = 1024
SUB_TILE = 256
COL_BLOCK = 256
SCRATCH_SLOTS = 2
PACK_BLOCK_ROWS = 256
VMEM_LIMIT_BYTES = 58 * 1024 * 1024


def _layernorm(x, g, b):
    mu = jnp.mean(x, axis=-1, keepdims=True)
    xc = x - mu
    var = jnp.mean(xc * xc, axis=-1, keepdims=True)
    return xc * jax.lax.rsqrt(var + LN_EPS) * g + b


def _silu(z):
    return z * jax.nn.sigmoid(z)


def _pack_kernel(*refs):
    n = len(refs) // 2
    for w_ref, o_ref in zip(refs[:n], refs[n:]):
        o_ref[...] = pltpu.bitcast(w_ref[...].astype(jnp.bfloat16), jnp.uint32)


def _pack_bf16(weights):
    rows = weights[0].shape[0]
    assert all(w.ndim == 2 and w.shape[0] == rows for w in weights) and rows % PACK_BLOCK_ROWS == 0
    return pl.pallas_call(
        _pack_kernel,
        grid=(rows // PACK_BLOCK_ROWS,),
        in_specs=[pl.BlockSpec((PACK_BLOCK_ROWS, w.shape[1]), lambda i: (i, 0)) for w in weights],
        out_specs=[pl.BlockSpec((PACK_BLOCK_ROWS // 2, w.shape[1]), lambda i: (i, 0)) for w in weights],
        out_shape=[jax.ShapeDtypeStruct((rows // 2, w.shape[1]), jnp.uint32) for w in weights],
        compiler_params=pltpu.CompilerParams(
            dimension_semantics=("arbitrary",), vmem_limit_bytes=VMEM_LIMIT_BYTES),
    )(*weights)


def _as_bf16(packed):
    return pltpu.bitcast(packed, jnp.bfloat16)


def _layer_kernel(xprev_ref, x_ref, xnext_ref, w_in_ref, b_in_ref, ln_v_g_ref, ln_v_b_ref, w_sp_ref,
                  b_sp_ref, w_pool_ref, b_pool_ref, pool_scale_ref, w_br_a_ref, w_br_b_ref, w_out_ref,
                  b_out_ref, ln_g_ref, ln_b_ref, band_ref, out_ref,
                  xext_ref, vn_ref, vs_ref, pext_ref, m_ref, wpp_ref, bpp_ref, bra_ref, brb_ref, merged_ref,
                  *, seq_len):
    tm = x_ref.shape[0]
    sub = vn_ref.shape[1]
    n_sub = tm // sub
    n_chunks = sub // CHUNK
    tiles_per_seq = seq_len // tm
    j = pl.program_id(0) % tiles_per_seq
    is_first = j == 0
    is_last = j == tiles_per_seq - 1

    def in_proj(lhs, col, width=D_MODEL):
        return (jnp.dot(lhs, _as_bf16(w_in_ref[:, col:col + width]), preferred_element_type=jnp.float32)
                + b_in_ref[:, col:col + width])

    @pl.when(pl.program_id(0) == 0)
    def _():
        for g in range(len(POOL_WINDOWS)):
            cols = slice(g * POOL_GROUP_W, (g + 1) * POOL_GROUP_W)
            pcols = slice(COL_P + g * POOL_GROUP_W, COL_P + (g + 1) * POOL_GROUP_W)
            w_pool_g = _as_bf16(w_pool_ref[g])
            wpp_ref[:, cols] = jnp.dot(_as_bf16(w_in_ref[:, pcols]), w_pool_g,
                                       preferred_element_type=jnp.float32).astype(jnp.bfloat16)
            b_p = jnp.broadcast_to(b_in_ref[:, pcols], bpp_ref[:, cols].shape).astype(jnp.bfloat16)
            bpp_ref[:, cols] = jnp.dot(b_p, w_pool_g, preferred_element_type=jnp.float32)

    def sub_tile(s):
        r0 = s * sub
        slot = s % vn_ref.shape[0]
        ext_rows = slice(0 if s == 0 else r0 + 2 * HALO, r0 + sub + 2 * HALO)
        if s == 0:
            xext_ref[0:HALO, :] = xprev_ref[...].astype(jnp.bfloat16)
        own = slice(max(ext_rows.start - HALO, 0), min(ext_rows.stop - HALO, tm))
        xext_ref[HALO + own.start:HALO + own.stop, :] = x_ref[own, :].astype(jnp.bfloat16)
        if s == n_sub - 1:
            xext_ref[HALO + tm:, :] = xnext_ref[...].astype(jnp.bfloat16)
        xb = xext_ref[HALO + r0:HALO + r0 + sub, :]

        v = in_proj(xb, COL_V)
        vn_ref[slot] = _layernorm(v, ln_v_g_ref[...], ln_v_b_ref[...]).astype(jnp.bfloat16)
        yield

        pext_ref[ext_rows, :] = (jnp.dot(xext_ref[ext_rows, :], wpp_ref[...], preferred_element_type=jnp.float32)
                                 + bpp_ref[0:1, :]).astype(jnp.bfloat16)
        for n in range(0, n_chunks, 2):
            rows0 = slice(n * CHUNK, (n + 1) * CHUNK)
            rows1 = slice((n + 1) * CHUNK, (n + 2) * CHUNK)
            for g in range(A_GROUPS):
                cols = slice(g * A_GROUP_W, (g + 1) * A_GROUP_W)
                rhs = jnp.concatenate([vn_ref[slot, rows0, cols], vn_ref[slot, rows1, cols]], axis=1)
                mixed = jnp.dot(_as_bf16(w_sp_ref[g]), rhs, preferred_element_type=jnp.float32)
                vs_ref[slot, rows0, cols] = mixed[:, :A_GROUP_W] + b_sp_ref[:, cols]
                vs_ref[slot, rows1, cols] = mixed[:, A_GROUP_W:] + b_sp_ref[:, cols]
        for c in range(n_chunks):
            variant = jnp.where(jnp.logical_and(is_first, s == 0 and c == 0), 0,
                                jnp.where(jnp.logical_and(is_last, s == n_sub - 1 and c == n_chunks - 1), 2, 1))
            for g in range(len(POOL_WINDOWS)):
                cols = slice(g * POOL_GROUP_W, (g + 1) * POOL_GROUP_W)
                m_ref[slot, c * CHUNK:(c + 1) * CHUNK, cols] = (
                    jnp.dot(_as_bf16(band_ref[variant, g]), pext_ref[r0 + c * CHUNK:r0 + c * CHUNK + BAND_K, cols],
                            preferred_element_type=jnp.float32) + b_pool_ref[:, cols])

        for n in range(D_MODEL // COL_BLOCK):
            cols = slice(n * COL_BLOCK, (n + 1) * COL_BLOCK)
            u = in_proj(xb, COL_U + n * COL_BLOCK, COL_BLOCK)
            z_a = in_proj(xb, COL_ZA + n * COL_BLOCK, COL_BLOCK)
            bra_ref[slot, :, cols] = (u * vs_ref[slot, :, cols] * _silu(z_a)).astype(jnp.bfloat16)
            z_b = in_proj(xb, COL_ZB + n * COL_BLOCK, COL_BLOCK)
            brb_ref[slot, :, cols] = (m_ref[slot, :, cols] * pool_scale_ref[:, cols] * _silu(z_b)).astype(jnp.bfloat16)
        resid = DEEPNORM_ALPHA * x_ref[r0:r0 + sub, :] + b_out_ref[...]
        for n in range(D_MODEL // COL_BLOCK):
            cols = slice(n * COL_BLOCK, (n + 1) * COL_BLOCK)
            g_a = in_proj(xb, COL_GA + n * COL_BLOCK, COL_BLOCK)
            proj_a = jnp.dot(bra_ref[slot], _as_bf16(w_br_a_ref[:, cols]), preferred_element_type=jnp.float32)
            g_b = in_proj(xb, COL_GB + n * COL_BLOCK, COL_BLOCK)
            proj_b = jnp.dot(brb_ref[slot], _as_bf16(w_br_b_ref[:, cols]), preferred_element_type=jnp.float32)
            merged_ref[slot, :, cols] = (jax.nn.sigmoid(g_a) * proj_a
                                         + jax.nn.sigmoid(g_b) * proj_b).astype(jnp.bfloat16)
        yield

        out = jnp.dot(merged_ref[slot], _as_bf16(w_out_ref[...]), preferred_element_type=jnp.float32)
        out_ref[r0:r0 + sub, :] = _layernorm(resid + out, ln_g_ref[...], ln_b_ref[...])
        yield

    tiles = [sub_tile(s) for s in range(n_sub)]
    next(tiles[0])
    for s in range(n_sub):
        next(tiles[s])
        if s + 1 < n_sub:
            next(tiles[s + 1])
        next(tiles[s])


def _pool_band_table():
    t = np.arange(CHUNK)[:, None]
    k = np.arange(BAND_K)[None, :] - HALO
    table = np.zeros((3, len(POOL_WINDOWS), CHUNK, BAND_K), np.float32)
    for variant, valid in enumerate((k >= 0, np.ones_like(k, bool), k < CHUNK)):
        for g, w in enumerate(POOL_WINDOWS):
            in_window = (k >= t - w // 2) & (k < t + w - w // 2) & valid
            table[variant, g] = in_window / in_window.sum(axis=1, keepdims=True) - (k == t)
    bits = table.astype(jnp.bfloat16).view(np.uint16).astype(np.uint32)
    return jnp.asarray(bits[..., 0::2, :] | (bits[..., 1::2, :] << 16))


def _resident(shape):
    return pl.BlockSpec(shape, lambda i: (0,) * len(shape), pipeline_mode=pl.Buffered(1))


def _run_layer(x, params, tm=TOKEN_TILE, sub=SUB_TILE):
    bsz, seq_len, d = x.shape
    assert d == D_MODEL and seq_len % tm == 0 and tm % sub == 0 and sub % (2 * CHUNK) == 0
    assert seq_len >= 2 * CHUNK
    n_tok = bsz * seq_len
    x2 = x.reshape(n_tok, d)
    halo_per_tile = tm // HALO
    n_halo_blocks = n_tok // HALO

    in_specs = [
        pl.BlockSpec((HALO, d), lambda i: (jnp.maximum(i * halo_per_tile - 1, 0), 0)),
        pl.BlockSpec((tm, d), lambda i: (i, 0)),
        pl.BlockSpec((HALO, d), lambda i: (jnp.minimum((i + 1) * halo_per_tile, n_halo_blocks - 1), 0)),
    ] + [_resident(p.shape) for p in params]

    scratch = [
        pltpu.VMEM((tm + 2 * HALO, d), jnp.bfloat16),
        pltpu.VMEM((SCRATCH_SLOTS, sub, d), jnp.bfloat16),
        pltpu.VMEM((SCRATCH_SLOTS, sub, d), jnp.float32),
        pltpu.VMEM((tm + 2 * HALO, d), jnp.bfloat16),
        pltpu.VMEM((SCRATCH_SLOTS, sub, d), jnp.float32),
        pltpu.VMEM((d, d), jnp.bfloat16),
        pltpu.VMEM((HALO, d), jnp.float32),
        pltpu.VMEM((SCRATCH_SLOTS, sub, d), jnp.bfloat16),
        pltpu.VMEM((SCRATCH_SLOTS, sub, d), jnp.bfloat16),
        pltpu.VMEM((SCRATCH_SLOTS, sub, d), jnp.bfloat16),
    ]
    y = pl.pallas_call(
        functools.partial(_layer_kernel, seq_len=seq_len),
        grid=(n_tok // tm,),
        in_specs=in_specs,
        out_specs=pl.BlockSpec((tm, d), lambda i: (i, 0)),
        out_shape=jax.ShapeDtypeStruct((n_tok, d), x.dtype),
        scratch_shapes=scratch,
        compiler_params=pltpu.CompilerParams(
            dimension_semantics=("arbitrary",), vmem_limit_bytes=VMEM_LIMIT_BYTES),
    )(x2, x2, x2, *params)
    return y.reshape(bsz, seq_len, d)


def kernel(x_prompt, x_sample, w_in, b_in, ln_v_g, ln_v_b, w_spatial, b_spatial, w_pool, b_pool, pool_scale,
           w_br_a, w_br_b, w_out, b_out, ln_g, ln_b):
    assert w_in.shape[0] == DEPTH
    row = lambda a: a[0].reshape(1, -1)
    b_sp = jnp.repeat(b_spatial[0].T, A_GROUP_W, axis=1)
    w_in_p, w_sp_p, w_pool_p, w_br_a_p, w_br_b_p, w_out_p = _pack_bf16([
        w_in[0], w_spatial[0].reshape(A_GROUPS * CHUNK, CHUNK),
        w_pool[0].reshape(len(POOL_WINDOWS) * POOL_GROUP_W, POOL_GROUP_W), w_br_a[0], w_br_b[0], w_out[0]])
    params = (
        w_in_p, row(b_in), row(ln_v_g), row(ln_v_b), w_sp_p.reshape(A_GROUPS, CHUNK // 2, CHUNK), b_sp,
        w_pool_p.reshape(len(POOL_WINDOWS), POOL_GROUP_W // 2, POOL_GROUP_W), row(b_pool), row(pool_scale),
        w_br_a_p, w_br_b_p, w_out_p, row(b_out), row(ln_g), row(ln_b), _pool_band_table(),
    )
    return _run_layer(x_prompt, params), _run_layer(x_sample, params)
```

```python
import functools

import numpy as np
import jax
import jax.numpy as jnp
from jax.experimental import pallas as pl
from jax.experimental.pallas import tpu as pltpu

D_MODEL = 1024
CHUNK = 128
A_GROUPS = 8
A_GROUP_W = D_MODEL // A_GROUPS
POOL_WINDOWS = (2, 4, 8, 16)
POOL_GROUP_W = D_MODEL // len(POOL_WINDOWS)
DEPTH = 1
DEEPNORM_ALPHA = (2.0 * DEPTH) ** 0.25
LN_EPS = 1e-5

COL_U, COL_V, COL_ZA, COL_P, COL_ZB, COL_GA, COL_GB = (i * D_MODEL for i in range(7))

HALO = 16
BAND_K = CHUNK + 2 * HALO
TOKEN_TILE = 1024
SUB_TILE = 256
COL_BLOCK = 256
SCRATCH_SLOTS = 2
PACK_BLOCK_ROWS = 256
V7X_VMEM_BYTES = 64 * 1024 * 1024
VMEM_LIMIT_BYTES = V7X_VMEM_BYTES - 6 * 1024 * 1024


def _layernorm(x, g, b):
    mu = jnp.mean(x, axis=-1, keepdims=True)
    xc = x - mu
    var = jnp.mean(xc * xc, axis=-1, keepdims=True)
    return xc * jax.lax.rsqrt(var + LN_EPS) * g + b


def _silu(z):
    return z * jax.nn.sigmoid(z)


def _pack_kernel(*refs):
    n = len(refs) // 2
    for w_ref, o_ref in zip(refs[:n], refs[n:]):
        o_ref[...] = pltpu.bitcast(w_ref[...].astype(jnp.bfloat16), jnp.uint32)


def _pack_bf16(weights):
    rows = weights[0].shape[0]
    assert all(w.ndim == 2 and w.shape[0] == rows for w in weights) and rows % PACK_BLOCK_ROWS == 0
    return pl.pallas_call(
        _pack_kernel,
        grid=(rows // PACK_BLOCK_ROWS,),
        in_specs=[pl.BlockSpec((PACK_BLOCK_ROWS, w.shape[1]), lambda i: (i, 0)) for w in weights],
        out_specs=[pl.BlockSpec((PACK_BLOCK_ROWS // 2, w.shape[1]), lambda i: (i, 0)) for w in weights],
        out_shape=[jax.ShapeDtypeStruct((rows // 2, w.shape[1]), jnp.uint32) for w in weights],
        compiler_params=pltpu.CompilerParams(
            dimension_semantics=("arbitrary",), vmem_limit_bytes=VMEM_LIMIT_BYTES),
    )(*weights)


def _as_bf16(packed):
    return pltpu.bitcast(packed, jnp.bfloat16)


def _layer_kernel(xprev_ref, x_ref, xnext_ref, w_in_ref, b_in_ref, ln_v_g_ref, ln_v_b_ref, w_sp_ref,
                  b_sp_ref, w_pool_ref, b_pool_ref, pool_scale_ref, w_br_a_ref, w_br_b_ref, w_out_ref,
                  b_out_ref, ln_g_ref, ln_b_ref, band_ref, out_ref,
                  xext_ref, vn_ref, vs_ref, pext_ref, m_ref, wpp_ref, bpp_ref, bra_ref, brb_ref, merged_ref,
                  *, seq_len):
    tm = x_ref.shape[0]
    sub = vn_ref.shape[1]
    n_sub = tm // sub
    n_chunks = sub // CHUNK
    tiles_per_seq = seq_len // tm
    j = pl.program_id(0) % tiles_per_seq
    is_first = j == 0
    is_last = j == tiles_per_seq - 1

    def in_proj(lhs, col, width=D_MODEL):
        return (jnp.dot(lhs, _as_bf16(w_in_ref[:, col:col + width]), preferred_element_type=jnp.float32)
                + b_in_ref[:, col:col + width])

    @pl.when(pl.program_id(0) == 0)
    def _():
        for g in range(len(POOL_WINDOWS)):
            cols = slice(g * POOL_GROUP_W, (g + 1) * POOL_GROUP_W)
            pcols = slice(COL_P + g * POOL_GROUP_W, COL_P + (g + 1) * POOL_GROUP_W)
            w_pool_g = _as_bf16(w_pool_ref[g])
            wpp_ref[:, cols] = jnp.dot(_as_bf16(w_in_ref[:, pcols]), w_pool_g,
                                       preferred_element_type=jnp.float32).astype(jnp.bfloat16)
            b_p = jnp.broadcast_to(b_in_ref[:, pcols], bpp_ref[:, cols].shape).astype(jnp.bfloat16)
            bpp_ref[:, cols] = jnp.dot(b_p, w_pool_g, preferred_element_type=jnp.float32)

    def sub_tile(s):
        r0 = s * sub
        slot = s % vn_ref.shape[0]
        ext_rows = slice(0 if s == 0 else r0 + 2 * HALO, r0 + sub + 2 * HALO)
        if s == 0:
            xext_ref[0:HALO, :] = xprev_ref[...].astype(jnp.bfloat16)
        own = slice(max(ext_rows.start - HALO, 0), min(ext_rows.stop - HALO, tm))
        xext_ref[HALO + own.start:HALO + own.stop, :] = x_ref[own, :].astype(jnp.bfloat16)
        if s == n_sub - 1:
            xext_ref[HALO + tm:, :] = xnext_ref[...].astype(jnp.bfloat16)
        xb = xext_ref[HALO + r0:HALO + r0 + sub, :]

        v = in_proj(xb, COL_V)
        vn_ref[slot] = _layernorm(v, ln_v_g_ref[...], ln_v_b_ref[...]).astype(jnp.bfloat16)
        yield

        pext_ref[ext_rows, :] = (jnp.dot(xext_ref[ext_rows, :], wpp_ref[...], preferred_element_type=jnp.float32)
                                 + bpp_ref[0:1, :]).astype(jnp.bfloat16)
        for n in range(0, n_chunks, 2):
            rows0 = slice(n * CHUNK, (n + 1) * CHUNK)
            rows1 = slice((n + 1) * CHUNK, (n + 2) * CHUNK)
            for g in range(A_GROUPS):
                cols = slice(g * A_GROUP_W, (g + 1) * A_GROUP_W)
                rhs = jnp.concatenate([vn_ref[slot, rows0, cols], vn_ref[slot, rows1, cols]], axis=1)
                mixed = jnp.dot(_as_bf16(w_sp_ref[g]), rhs, preferred_element_type=jnp.float32)
                vs_ref[slot, rows0, cols] = mixed[:, :A_GROUP_W] + b_sp_ref[:, cols]
                vs_ref[slot, rows1, cols] = mixed[:, A_GROUP_W:] + b_sp_ref[:, cols]
        for c in range(n_chunks):
            variant = jnp.where(jnp.logical_and(is_first, s == 0 and c == 0), 0,
                                jnp.where(jnp.logical_and(is_last, s == n_sub - 1 and c == n_chunks - 1), 2, 1))
            for g in range(len(POOL_WINDOWS)):
                cols = slice(g * POOL_GROUP_W, (g + 1) * POOL_GROUP_W)
                m_ref[slot, c * CHUNK:(c + 1) * CHUNK, cols] = (
                    jnp.dot(_as_bf16(band_ref[variant, g]), pext_ref[r0 + c * CHUNK:r0 + c * CHUNK + BAND_K, cols],
                            preferred_element_type=jnp.float32) + b_pool_ref[:, cols])

        for n in range(D_MODEL // COL_BLOCK):
            cols = slice(n * COL_BLOCK, (n + 1) * COL_BLOCK)
            z_a = in_proj(xb, COL_ZA + n * COL_BLOCK, COL_BLOCK)
            u = in_proj(xb, COL_U + n * COL_BLOCK, COL_BLOCK)
            bra_ref[slot, :, cols] = (u * vs_ref[slot, :, cols] * _silu(z_a)).astype(jnp.bfloat16)
            z_b = in_proj(xb, COL_ZB + n * COL_BLOCK, COL_BLOCK)
            brb_ref[slot, :, cols] = (m_ref[slot, :, cols] * pool_scale_ref[:, cols] * _silu(z_b)).astype(jnp.bfloat16)
        resid = DEEPNORM_ALPHA * x_ref[r0:r0 + sub, :] + b_out_ref[...]
        for n in range(D_MODEL // COL_BLOCK):
            cols = slice(n * COL_BLOCK, (n + 1) * COL_BLOCK)
            g_a = in_proj(xb, COL_GA + n * COL_BLOCK, COL_BLOCK)
            proj_a = jnp.dot(bra_ref[slot], _as_bf16(w_br_a_ref[:, cols]), preferred_element_type=jnp.float32)
            g_b = in_proj(xb, COL_GB + n * COL_BLOCK, COL_BLOCK)
            proj_b = jnp.dot(brb_ref[slot], _as_bf16(w_br_b_ref[:, cols]), preferred_element_type=jnp.float32)
            merged_ref[slot, :, cols] = (jax.nn.sigmoid(g_a) * proj_a
                                         + jax.nn.sigmoid(g_b) * proj_b).astype(jnp.bfloat16)
        yield

        out = jnp.dot(merged_ref[slot], _as_bf16(w_out_ref[...]), preferred_element_type=jnp.float32)
        out_ref[r0:r0 + sub, :] = _layernorm(resid + out, ln_g_ref[...], ln_b_ref[...])
        yield

    tiles = [sub_tile(s) for s in range(n_sub)]
    next(tiles[0])
    for s in range(n_sub):
        next(tiles[s])
        if s + 1 < n_sub:
            next(tiles[s + 1])
        next(tiles[s])


def _pool_band_table():
    t = np.arange(CHUNK)[:, None]
    k = np.arange(BAND_K)[None, :] - HALO
    table = np.zeros((3, len(POOL_WINDOWS), CHUNK, BAND_K), np.float32)
    for variant, valid in enumerate((k >= 0, np.ones_like(k, bool), k < CHUNK)):
        for g, w in enumerate(POOL_WINDOWS):
            in_window = (k >= t - w // 2) & (k < t + w - w // 2) & valid
            table[variant, g] = in_window / in_window.sum(axis=1, keepdims=True) - (k == t)
    bits = table.astype(jnp.bfloat16).view(np.uint16).astype(np.uint32)
    return jnp.asarray(bits[..., 0::2, :] | (bits[..., 1::2, :] << 16))


def _resident(shape):
    return pl.BlockSpec(shape, lambda i: (0,) * len(shape), pipeline_mode=pl.Buffered(1))


def _run_layer(x, params, tm=TOKEN_TILE, sub=SUB_TILE):
    bsz, seq_len, d = x.shape
    assert d == D_MODEL and seq_len % tm == 0 and tm % sub == 0 and sub % (2 * CHUNK) == 0
    assert seq_len >= 2 * CHUNK
    n_tok = bsz * seq_len
    x2 = x.reshape(n_tok, d)
    halo_per_tile = tm // HALO
    n_halo_blocks = n_tok // HALO

    in_specs = [
        pl.BlockSpec((HALO, d), lambda i: (jnp.maximum(i * halo_per_tile - 1, 0), 0)),
        pl.BlockSpec((tm, d), lambda i: (i, 0)),
        pl.BlockSpec((HALO, d), lambda i: (jnp.minimum((i + 1) * halo_per_tile, n_halo_blocks - 1), 0)),
    ] + [_resident(p.shape) for p in params]

    scratch = [
        pltpu.VMEM((tm + 2 * HALO, d), jnp.bfloat16),
        pltpu.VMEM((SCRATCH_SLOTS, sub, d), jnp.bfloat16),
        pltpu.VMEM((SCRATCH_SLOTS, sub, d), jnp.float32),
        pltpu.VMEM((tm + 2 * HALO, d), jnp.bfloat16),
        pltpu.VMEM((SCRATCH_SLOTS, sub, d), jnp.float32),
        pltpu.VMEM((d, d), jnp.bfloat16),
        pltpu.VMEM((HALO, d), jnp.float32),
        pltpu.VMEM((SCRATCH_SLOTS, sub, d), jnp.bfloat16),
        pltpu.VMEM((SCRATCH_SLOTS, sub, d), jnp.bfloat16),
        pltpu.VMEM((SCRATCH_SLOTS, sub, d), jnp.bfloat16),
    ]
    y = pl.pallas_call(
        functools.partial(_layer_kernel, seq_len=seq_len),
        grid=(n_tok // tm,),
        in_specs=in_specs,
        out_specs=pl.BlockSpec((tm, d), lambda i: (i, 0)),
        out_shape=jax.ShapeDtypeStruct((n_tok, d), x.dtype),
        scratch_shapes=scratch,
        compiler_params=pltpu.CompilerParams(
            dimension_semantics=("arbitrary",), vmem_limit_bytes=VMEM_LIMIT_BYTES),
    )(x2, x2, x2, *params)
    return y.reshape(bsz, seq_len, d)


def kernel(x_prompt, x_sample, w_in, b_in, ln_v_g, ln_v_b, w_spatial, b_spatial, w_pool, b_pool, pool_scale,
           w_br_a, w_br_b, w_out, b_out, ln_g, ln_b):
    assert w_in.shape[0] == DEPTH
    row = lambda a: a[0].reshape(1, -1)
    b_sp = jnp.repeat(b_spatial[0].T, A_GROUP_W, axis=1)
    w_in_p, w_sp_p, w_pool_p, w_br_a_p, w_br_b_p, w_out_p = _pack_bf16([
        w_in[0], w_spatial[0].reshape(A_GROUPS * CHUNK, CHUNK),
        w_pool[0].reshape(len(POOL_WINDOWS) * POOL_GROUP_W, POOL_GROUP_W), w_br_a[0], w_br_b[0], w_out[0]])
    params = (
        w_in_p, row(b_in), row(ln_v_g), row(ln_v_b), w_sp_p.reshape(A_GROUPS, CHUNK // 2, CHUNK), b_sp,
        w_pool_p.reshape(len(POOL_WINDOWS), POOL_GROUP_W // 2, POOL_GROUP_W), row(b_pool), row(pool_scale),
        w_br_a_p, w_br_b_p, w_out_p, row(b_out), row(ln_g), row(ln_b), _pool_band_table(),
    )
    y_sample = _run_layer(x_sample, params)
    return _run_layer(x_prompt, params), y_sample
```

```python
import functools

import numpy as np
import jax
import jax.numpy as jnp
from jax.experimental import pallas as pl
from jax.experimental.pallas import tpu as pltpu

D_MODEL = 1024
CHUNK = 128
A_GROUPS = 8
A_GROUP_W = D_MODEL // A_GROUPS
POOL_WINDOWS = (2, 4, 8, 16)
POOL_GROUP_W = D_MODEL // len(POOL_WINDOWS)
DEPTH = 1
DEEPNORM_ALPHA = (2.0 * DEPTH) ** 0.25
LN_EPS = 1e-5

COL_U, COL_V, COL_ZA, COL_P, COL_ZB, COL_GA, COL_GB = (i * D_MODEL for i in range(7))

HALO = 16
BAND_K = CHUNK + 2 * HALO
TOKEN_TILE = 1024
SUB_TILE = 512
COL_BLOCK = 256
SCRATCH_SLOTS = 2
PACK_BLOCK_ROWS = 256
V7X_VMEM_BYTES = 64 * 1024 * 1024
VMEM_LIMIT_BYTES = V7X_VMEM_BYTES - 6 * 1024 * 1024


def _layernorm(x, g, b):
    mu = jnp.mean(x, axis=-1, keepdims=True)
    xc = x - mu
    var = jnp.mean(xc * xc, axis=-1, keepdims=True)
    return xc * jax.lax.rsqrt(var + LN_EPS) * g + b


def _silu(z):
    return z * jax.nn.sigmoid(z)


def _pack_kernel(*refs):
    n = len(refs) // 2
    for w_ref, o_ref in zip(refs[:n], refs[n:]):
        o_ref[...] = pltpu.bitcast(w_ref[...].astype(jnp.bfloat16), jnp.uint32)


def _pack_bf16(weights):
    rows = weights[0].shape[0]
    assert all(w.ndim == 2 and w.shape[0] == rows for w in weights) and rows % PACK_BLOCK_ROWS == 0
    return pl.pallas_call(
        _pack_kernel,
        grid=(rows // PACK_BLOCK_ROWS,),
        in_specs=[pl.BlockSpec((PACK_BLOCK_ROWS, w.shape[1]), lambda i: (i, 0)) for w in weights],
        out_specs=[pl.BlockSpec((PACK_BLOCK_ROWS // 2, w.shape[1]), lambda i: (i, 0)) for w in weights],
        out_shape=[jax.ShapeDtypeStruct((rows // 2, w.shape[1]), jnp.uint32) for w in weights],
        compiler_params=pltpu.CompilerParams(
            dimension_semantics=("arbitrary",), vmem_limit_bytes=VMEM_LIMIT_BYTES),
    )(*weights)


def _as_bf16(packed):
    return pltpu.bitcast(packed, jnp.bfloat16)


def _layer_kernel(xprev_ref, x_ref, xnext_ref, w_in_ref, b_in_ref, ln_v_g_ref, ln_v_b_ref, w_sp_ref,
                  b_sp_ref, w_pool_ref, b_pool_ref, pool_scale_ref, w_br_a_ref, w_br_b_ref, w_out_ref,
                  b_out_ref, ln_g_ref, ln_b_ref, band_ref, out_ref,
                  xext_ref, vn_ref, vs_ref, pext_ref, m_ref, wpp_ref, bpp_ref, bra_ref, brb_ref, merged_ref,
                  *, seq_len):
    tm = x_ref.shape[0]
    sub = vn_ref.shape[1]
    n_sub = tm // sub
    n_chunks = sub // CHUNK
    tiles_per_seq = seq_len // tm
    j = pl.program_id(0) % tiles_per_seq
    is_first = j == 0
    is_last = j == tiles_per_seq - 1

    def in_proj(lhs, col, width=D_MODEL):
        return (jnp.dot(lhs, _as_bf16(w_in_ref[:, col:col + width]), preferred_element_type=jnp.float32)
                + b_in_ref[:, col:col + width])

    @pl.when(pl.program_id(0) == 0)
    def _():
        for g in range(len(POOL_WINDOWS)):
            cols = slice(g * POOL_GROUP_W, (g + 1) * POOL_GROUP_W)
            pcols = slice(COL_P + g * POOL_GROUP_W, COL_P + (g + 1) * POOL_GROUP_W)
            w_pool_g = _as_bf16(w_pool_ref[g])
            wpp_ref[:, cols] = jnp.dot(_as_bf16(w_in_ref[:, pcols]), w_pool_g,
                                       preferred_element_type=jnp.float32).astype(jnp.bfloat16)
            b_p = jnp.broadcast_to(b_in_ref[:, pcols], bpp_ref[:, cols].shape).astype(jnp.bfloat16)
            bpp_ref[:, cols] = jnp.dot(b_p, w_pool_g, preferred_element_type=jnp.float32)

    def sub_tile(s):
        r0 = s * sub
        slot = s % vn_ref.shape[0]
        ext_rows = slice(0 if s == 0 else r0 + 2 * HALO, r0 + sub + 2 * HALO)
        if s == 0:
            xext_ref[0:HALO, :] = xprev_ref[...].astype(jnp.bfloat16)
        own = slice(max(ext_rows.start - HALO, 0), min(ext_rows.stop - HALO, tm))
        xext_ref[HALO + own.start:HALO + own.stop, :] = x_ref[own, :].astype(jnp.bfloat16)
        if s == n_sub - 1:
            xext_ref[HALO + tm:, :] = xnext_ref[...].astype(jnp.bfloat16)
        xb = xext_ref[HALO + r0:HALO + r0 + sub, :]

        v = in_proj(xb, COL_V)
        vn_ref[slot] = _layernorm(v, ln_v_g_ref[...], ln_v_b_ref[...]).astype(jnp.bfloat16)
        yield

        pext_ref[ext_rows, :] = (jnp.dot(xext_ref[ext_rows, :], wpp_ref[...], preferred_element_type=jnp.float32)
                                 + bpp_ref[0:1, :]).astype(jnp.bfloat16)
        for n in range(0, n_chunks, 2):
            rows0 = slice(n * CHUNK, (n + 1) * CHUNK)
            rows1 = slice((n + 1) * CHUNK, (n + 2) * CHUNK)
            for g in range(A_GROUPS):
                cols = slice(g * A_GROUP_W, (g + 1) * A_GROUP_W)
                rhs = jnp.concatenate([vn_ref[slot, rows0, cols], vn_ref[slot, rows1, cols]], axis=1)
                mixed = jnp.dot(_as_bf16(w_sp_ref[g]), rhs, preferred_element_type=jnp.float32)
                vs_ref[slot, rows0, cols] = mixed[:, :A_GROUP_W] + b_sp_ref[:, cols]
                vs_ref[slot, rows1, cols] = mixed[:, A_GROUP_W:] + b_sp_ref[:, cols]
        for c in range(n_chunks):
            variant = jnp.where(jnp.logical_and(is_first, s == 0 and c == 0), 0,
                                jnp.where(jnp.logical_and(is_last, s == n_sub - 1 and c == n_chunks - 1), 2, 1))
            for g in range(len(POOL_WINDOWS)):
                cols = slice(g * POOL_GROUP_W, (g + 1) * POOL_GROUP_W)
                m_ref[slot, c * CHUNK:(c + 1) * CHUNK, cols] = (
                    jnp.dot(_as_bf16(band_ref[variant, g]), pext_ref[r0 + c * CHUNK:r0 + c * CHUNK + BAND_K, cols],
                            preferred_element_type=jnp.float32) + b_pool_ref[:, cols])

        for n in range(D_MODEL // COL_BLOCK):
            cols = slice(n * COL_BLOCK, (n + 1) * COL_BLOCK)
            z_a = in_proj(xb, COL_ZA + n * COL_BLOCK, COL_BLOCK)
            u = in_proj(xb, COL_U + n * COL_BLOCK, COL_BLOCK)
            bra_ref[slot, :, cols] = (u * vs_ref[slot, :, cols] * _silu(z_a)).astype(jnp.bfloat16)
            z_b = in_proj(xb, COL_ZB + n * COL_BLOCK, COL_BLOCK)
            brb_ref[slot, :, cols] = (m_ref[slot, :, cols] * pool_scale_ref[:, cols] * _silu(z_b)).astype(jnp.bfloat16)
        resid = DEEPNORM_ALPHA * x_ref[r0:r0 + sub, :] + b_out_ref[...]
        for n in range(D_MODEL // COL_BLOCK):
            cols = slice(n * COL_BLOCK, (n + 1) * COL_BLOCK)
            g_a = in_proj(xb, COL_GA + n * COL_BLOCK, COL_BLOCK)
            proj_a = jnp.dot(bra_ref[slot], _as_bf16(w_br_a_ref[:, cols]), preferred_element_type=jnp.float32)
            g_b = in_proj(xb, COL_GB + n * COL_BLOCK, COL_BLOCK)
            proj_b = jnp.dot(brb_ref[slot], _as_bf16(w_br_b_ref[:, cols]), preferred_element_type=jnp.float32)
            merged_ref[slot, :, cols] = (jax.nn.sigmoid(g_a) * proj_a
                                         + jax.nn.sigmoid(g_b) * proj_b).astype(jnp.bfloat16)
        yield

        out = jnp.dot(merged_ref[slot], _as_bf16(w_out_ref[...]), preferred_element_type=jnp.float32)
        out_ref[r0:r0 + sub, :] = _layernorm(resid + out, ln_g_ref[...], ln_b_ref[...])
        yield

    tiles = [sub_tile(s) for s in range(n_sub)]
    next(tiles[0])
    for s in range(n_sub):
        next(tiles[s])
        if s + 1 < n_sub:
            next(tiles[s + 1])
        next(tiles[s])


def _pool_band_table():
    t = np.arange(CHUNK)[:, None]
    k = np.arange(BAND_K)[None, :] - HALO
    table = np.zeros((3, len(POOL_WINDOWS), CHUNK, BAND_K), np.float32)
    for variant, valid in enumerate((k >= 0, np.ones_like(k, bool), k < CHUNK)):
        for g, w in enumerate(POOL_WINDOWS):
            in_window = (k >= t - w // 2) & (k < t + w - w // 2) & valid
            table[variant, g] = in_window / in_window.sum(axis=1, keepdims=True) - (k == t)
    bits = table.astype(jnp.bfloat16).view(np.uint16).astype(np.uint32)
    return jnp.asarray(bits[..., 0::2, :] | (bits[..., 1::2, :] << 16))


def _resident(shape):
    return pl.BlockSpec(shape, lambda i: (0,) * len(shape), pipeline_mode=pl.Buffered(1))


def _run_layer(x, params, tm=TOKEN_TILE, sub=SUB_TILE):
    bsz, seq_len, d = x.shape
    assert d == D_MODEL and seq_len % tm == 0 and tm % sub == 0 and sub % (2 * CHUNK) == 0
    assert seq_len >= 2 * CHUNK
    n_tok = bsz * seq_len
    x2 = x.reshape(n_tok, d)
    halo_per_tile = tm // HALO
    n_halo_blocks = n_tok // HALO

    in_specs = [
        pl.BlockSpec((HALO, d), lambda i: (jnp.maximum(i * halo_per_tile - 1, 0), 0)),
        pl.BlockSpec((tm, d), lambda i: (i, 0)),
        pl.BlockSpec((HALO, d), lambda i: (jnp.minimum((i + 1) * halo_per_tile, n_halo_blocks - 1), 0)),
    ] + [_resident(p.shape) for p in params]

    scratch = [
        pltpu.VMEM((tm + 2 * HALO, d), jnp.bfloat16),
        pltpu.VMEM((SCRATCH_SLOTS, sub, d), jnp.bfloat16),
        pltpu.VMEM((SCRATCH_SLOTS, sub, d), jnp.float32),
        pltpu.VMEM((tm + 2 * HALO, d), jnp.bfloat16),
        pltpu.VMEM((SCRATCH_SLOTS, sub, d), jnp.float32),
        pltpu.VMEM((d, d), jnp.bfloat16),
        pltpu.VMEM((HALO, d), jnp.float32),
        pltpu.VMEM((SCRATCH_SLOTS, sub, d), jnp.bfloat16),
        pltpu.VMEM((SCRATCH_SLOTS, sub, d), jnp.bfloat16),
        pltpu.VMEM((SCRATCH_SLOTS, sub, d), jnp.bfloat16),
    ]
    y = pl.pallas_call(
        functools.partial(_layer_kernel, seq_len=seq_len),
        grid=(n_tok // tm,),
        in_specs=in_specs,
        out_specs=pl.BlockSpec((tm, d), lambda i: (i, 0)),
        out_shape=jax.ShapeDtypeStruct((n_tok, d), x.dtype),
        scratch_shapes=scratch,
        compiler_params=pltpu.CompilerParams(
            dimension_semantics=("arbitrary",), vmem_limit_bytes=VMEM_LIMIT_BYTES),
    )(x2, x2, x2, *params)
    return y.reshape(bsz, seq_len, d)


def kernel(x_prompt, x_sample, w_in, b_in, ln_v_g, ln_v_b, w_spatial, b_spatial, w_pool, b_pool, pool_scale,
           w_br_a, w_br_b, w_out, b_out, ln_g, ln_b):
    assert w_in.shape[0] == DEPTH
    row = lambda a: a[0].reshape(1, -1)
    b_sp = jnp.repeat(b_spatial[0].T, A_GROUP_W, axis=1)
    w_in_p, w_sp_p, w_pool_p, w_br_a_p, w_br_b_p, w_out_p = _pack_bf16([
        w_in[0], w_spatial[0].reshape(A_GROUPS * CHUNK, CHUNK),
        w_pool[0].reshape(len(POOL_WINDOWS) * POOL_GROUP_W, POOL_GROUP_W), w_br_a[0], w_br_b[0], w_out[0]])
    params = (
        w_in_p, row(b_in), row(ln_v_g), row(ln_v_b), w_sp_p.reshape(A_GROUPS, CHUNK // 2, CHUNK), b_sp,
        w_pool_p.reshape(len(POOL_WINDOWS), POOL_GROUP_W // 2, POOL_GROUP_W), row(b_pool), row(pool_scale),
        w_br_a_p, w_br_b_p, w_out_p, row(b_out), row(ln_g), row(ln_b), _pool_band_table(),
    )
    y_sample = _run_layer(x_sample, params)
    return _run_layer(x_prompt, params), y_sample
```

```python
import functools

import numpy as np
import jax
import jax.numpy as jnp
from jax.experimental import pallas as pl
from jax.experimental.pallas import tpu as pltpu

D_MODEL = 1024
CHUNK = 128
A_GROUPS = 8
A_GROUP_W = D_MODEL // A_GROUPS
POOL_WINDOWS = (2, 4, 8, 16)
POOL_GROUP_W = D_MODEL // len(POOL_WINDOWS)
DEPTH = 1
DEEPNORM_ALPHA = (2.0 * DEPTH) ** 0.25
LN_EPS = 1e-5

COL_U, COL_V, COL_ZA, COL_P, COL_ZB, COL_GA, COL_GB = (i * D_MODEL for i in range(7))

HALO = 16
BAND_K = CHUNK + 2 * HALO
TOKEN_TILE = 1024
SUB_TILE = 256
COL_BLOCK = 256
SCRATCH_SLOTS = 2
PACK_BLOCK_ROWS = 256
V7X_VMEM_BYTES = 64 * 1024 * 1024
VMEM_LIMIT_BYTES = V7X_VMEM_BYTES - 6 * 1024 * 1024


def _layernorm(x, g, b):
    mu = jnp.mean(x, axis=-1, keepdims=True)
    xc = x - mu
    var = jnp.mean(xc * xc, axis=-1, keepdims=True)
    return xc * jax.lax.rsqrt(var + LN_EPS) * g + b


def _silu(z):
    return z * jax.nn.sigmoid(z)


def _pack_kernel(*refs):
    n = len(refs) // 2
    for w_ref, o_ref in zip(refs[:n], refs[n:]):
        o_ref[...] = pltpu.bitcast(w_ref[...].astype(jnp.bfloat16), jnp.uint32)


def _pack_bf16(weights):
    rows = weights[0].shape[0]
    assert all(w.ndim == 2 and w.shape[0] == rows for w in weights) and rows % PACK_BLOCK_ROWS == 0
    return pl.pallas_call(
        _pack_kernel,
        grid=(rows // PACK_BLOCK_ROWS,),
        in_specs=[pl.BlockSpec((PACK_BLOCK_ROWS, w.shape[1]), lambda i: (i, 0)) for w in weights],
        out_specs=[pl.BlockSpec((PACK_BLOCK_ROWS // 2, w.shape[1]), lambda i: (i, 0)) for w in weights],
        out_shape=[jax.ShapeDtypeStruct((rows // 2, w.shape[1]), jnp.uint32) for w in weights],
        compiler_params=pltpu.CompilerParams(
            dimension_semantics=("arbitrary",), vmem_limit_bytes=VMEM_LIMIT_BYTES),
    )(*weights)


def _as_bf16(packed):
    return pltpu.bitcast(packed, jnp.bfloat16)


def _fold_pool_linear(w_in_ref, b_in_ref, w_pool_ref, wpp_ref, bpp_ref):
    for g in range(len(POOL_WINDOWS)):
        cols = slice(g * POOL_GROUP_W, (g + 1) * POOL_GROUP_W)
        pcols = slice(COL_P + g * POOL_GROUP_W, COL_P + (g + 1) * POOL_GROUP_W)
        w_pool_g = _as_bf16(w_pool_ref[g])
        wpp_ref[:, cols] = jnp.dot(_as_bf16(w_in_ref[:, pcols]), w_pool_g,
                                   preferred_element_type=jnp.float32).astype(jnp.bfloat16)
        b_p = jnp.broadcast_to(b_in_ref[:, pcols], bpp_ref[:, cols].shape).astype(jnp.bfloat16)
        bpp_ref[:, cols] = jnp.dot(b_p, w_pool_g, preferred_element_type=jnp.float32)


def _layer_step(tile, tiles_per_seq, xprev_ref, x_ref, xnext_ref, out_ref,
                w_in_ref, b_in_ref, ln_v_g_ref, ln_v_b_ref, w_sp_ref,
                b_sp_ref, w_pool_ref, b_pool_ref, pool_scale_ref, w_br_a_ref, w_br_b_ref, w_out_ref,
                b_out_ref, ln_g_ref, ln_b_ref, band_ref,
                xext_ref, vn_ref, vs_ref, pext_ref, m_ref, wpp_ref, bpp_ref, bra_ref, brb_ref, merged_ref):
    tm = x_ref.shape[0]
    sub = vn_ref.shape[1]
    n_sub = tm // sub
    n_chunks = sub // CHUNK
    j = tile % tiles_per_seq
    is_first = j == 0
    is_last = j == tiles_per_seq - 1

    def in_proj(lhs, col, width=D_MODEL):
        return (jnp.dot(lhs, _as_bf16(w_in_ref[:, col:col + width]), preferred_element_type=jnp.float32)
                + b_in_ref[:, col:col + width])

    def sub_tile(s):
        r0 = s * sub
        slot = s % vn_ref.shape[0]
        ext_rows = slice(0 if s == 0 else r0 + 2 * HALO, r0 + sub + 2 * HALO)
        if s == 0:
            xext_ref[0:HALO, :] = xprev_ref[...].astype(jnp.bfloat16)
        own = slice(max(ext_rows.start - HALO, 0), min(ext_rows.stop - HALO, tm))
        xext_ref[HALO + own.start:HALO + own.stop, :] = x_ref[own, :].astype(jnp.bfloat16)
        if s == n_sub - 1:
            xext_ref[HALO + tm:, :] = xnext_ref[...].astype(jnp.bfloat16)
        xb = xext_ref[HALO + r0:HALO + r0 + sub, :]

        v = in_proj(xb, COL_V)
        vn_ref[slot] = _layernorm(v, ln_v_g_ref[...], ln_v_b_ref[...]).astype(jnp.bfloat16)
        yield

        pext_ref[ext_rows, :] = (jnp.dot(xext_ref[ext_rows, :], wpp_ref[...], preferred_element_type=jnp.float32)
                                 + bpp_ref[0:1, :]).astype(jnp.bfloat16)
        for n in range(0, n_chunks, 2):
            rows0 = slice(n * CHUNK, (n + 1) * CHUNK)
            rows1 = slice((n + 1) * CHUNK, (n + 2) * CHUNK)
            for g in range(A_GROUPS):
                cols = slice(g * A_GROUP_W, (g + 1) * A_GROUP_W)
                rhs = jnp.concatenate([vn_ref[slot, rows0, cols], vn_ref[slot, rows1, cols]], axis=1)
                mixed = jnp.dot(_as_bf16(w_sp_ref[g]), rhs, preferred_element_type=jnp.float32)
                vs_ref[slot, rows0, cols] = mixed[:, :A_GROUP_W] + b_sp_ref[:, cols]
                vs_ref[slot, rows1, cols] = mixed[:, A_GROUP_W:] + b_sp_ref[:, cols]
        for c in range(n_chunks):
            variant = jnp.where(jnp.logical_and(is_first, s == 0 and c == 0), 0,
                                jnp.where(jnp.logical_and(is_last, s == n_sub - 1 and c == n_chunks - 1), 2, 1))
            for g in range(len(POOL_WINDOWS)):
                cols = slice(g * POOL_GROUP_W, (g + 1) * POOL_GROUP_W)
                m_ref[slot, c * CHUNK:(c + 1) * CHUNK, cols] = (
                    jnp.dot(_as_bf16(band_ref[variant, g]), pext_ref[r0 + c * CHUNK:r0 + c * CHUNK + BAND_K, cols],
                            preferred_element_type=jnp.float32) + b_pool_ref[:, cols])

        for n in range(D_MODEL // COL_BLOCK):
            cols = slice(n * COL_BLOCK, (n + 1) * COL_BLOCK)
            z_a = in_proj(xb, COL_ZA + n * COL_BLOCK, COL_BLOCK)
            u = in_proj(xb, COL_U + n * COL_BLOCK, COL_BLOCK)
            bra_ref[slot, :, cols] = (u * vs_ref[slot, :, cols] * _silu(z_a)).astype(jnp.bfloat16)
            z_b = in_proj(xb, COL_ZB + n * COL_BLOCK, COL_BLOCK)
            brb_ref[slot, :, cols] = (m_ref[slot, :, cols] * pool_scale_ref[:, cols] * _silu(z_b)).astype(jnp.bfloat16)
        resid = DEEPNORM_ALPHA * x_ref[r0:r0 + sub, :] + b_out_ref[...]
        for n in range(D_MODEL // COL_BLOCK):
            cols = slice(n * COL_BLOCK, (n + 1) * COL_BLOCK)
            g_a = in_proj(xb, COL_GA + n * COL_BLOCK, COL_BLOCK)
            proj_a = jnp.dot(bra_ref[slot], _as_bf16(w_br_a_ref[:, cols]), preferred_element_type=jnp.float32)
            g_b = in_proj(xb, COL_GB + n * COL_BLOCK, COL_BLOCK)
            proj_b = jnp.dot(brb_ref[slot], _as_bf16(w_br_b_ref[:, cols]), preferred_element_type=jnp.float32)
            merged_ref[slot, :, cols] = (jax.nn.sigmoid(g_a) * proj_a
                                         + jax.nn.sigmoid(g_b) * proj_b).astype(jnp.bfloat16)
        yield

        out = jnp.dot(merged_ref[slot], _as_bf16(w_out_ref[...]), preferred_element_type=jnp.float32)
        out_ref[r0:r0 + sub, :] = _layernorm(resid + out, ln_g_ref[...], ln_b_ref[...])
        yield

    tiles = [sub_tile(s) for s in range(n_sub)]
    next(tiles[0])
    for s in range(n_sub):
        next(tiles[s])
        if s + 1 < n_sub:
            next(tiles[s + 1])
        next(tiles[s])


def _pool_band_table():
    t = np.arange(CHUNK)[:, None]
    k = np.arange(BAND_K)[None, :] - HALO
    table = np.zeros((3, len(POOL_WINDOWS), CHUNK, BAND_K), np.float32)
    for variant, valid in enumerate((k >= 0, np.ones_like(k, bool), k < CHUNK)):
        for g, w in enumerate(POOL_WINDOWS):
            in_window = (k >= t - w // 2) & (k < t + w - w // 2) & valid
            table[variant, g] = in_window / in_window.sum(axis=1, keepdims=True) - (k == t)
    bits = table.astype(jnp.bfloat16).view(np.uint16).astype(np.uint32)
    return jnp.asarray(bits[..., 0::2, :] | (bits[..., 1::2, :] << 16))


def _layers_kernel(*refs, seq_lens, tm):
    n_streams = len(seq_lens)
    xs = refs[:n_streams]
    n_params = 16
    params = refs[n_streams:n_streams + n_params]
    ys = refs[n_streams + n_params:2 * n_streams + n_params]
    scratch = refs[2 * n_streams + n_params:-1]
    step_ref = refs[-1]
    w_in_ref, b_in_ref, w_pool_ref = params[0], params[1], params[6]
    wpp_ref, bpp_ref = scratch[5], scratch[6]
    _fold_pool_linear(w_in_ref, b_in_ref, w_pool_ref, wpp_ref, bpp_ref)

    d = D_MODEL
    halo_per_tile = tm // HALO
    for x2, y2, seq_len in zip(xs, ys, seq_lens):
        n_tok = x2.shape[0]
        last_halo_block = n_tok // HALO - 1
        step_ref[0] = 0

        def step(xprev_ref, x_ref, xnext_ref, out_ref, seq_len=seq_len):
            tile = step_ref[0]
            step_ref[0] = tile + 1
            _layer_step(tile, seq_len // tm, xprev_ref, x_ref, xnext_ref, out_ref, *params, *scratch)

        pltpu.emit_pipeline(
            step,
            grid=(n_tok // tm,),
            in_specs=[
                pl.BlockSpec((HALO, d), lambda i: (jnp.maximum(i * halo_per_tile - 1, 0), 0)),
                pl.BlockSpec((tm, d), lambda i: (i, 0)),
                pl.BlockSpec((HALO, d), lambda i, last=last_halo_block: (
                    jnp.minimum((i + 1) * halo_per_tile, last), 0)),
            ],
            out_specs=[pl.BlockSpec((tm, d), lambda i: (i, 0))],
        )(x2, x2, x2, y2)


def _run_layers(streams, params, tm=TOKEN_TILE, sub=SUB_TILE):
    d = D_MODEL
    assert tm % sub == 0 and sub % (2 * CHUNK) == 0
    for x in streams:
        assert x.shape[2] == d and x.shape[1] % tm == 0
        assert x.shape[1] >= 2 * CHUNK
    scratch = [
        pltpu.VMEM((tm + 2 * HALO, d), jnp.bfloat16),
        pltpu.VMEM((SCRATCH_SLOTS, sub, d), jnp.bfloat16),
        pltpu.VMEM((SCRATCH_SLOTS, sub, d), jnp.float32),
        pltpu.VMEM((tm + 2 * HALO, d), jnp.bfloat16),
        pltpu.VMEM((SCRATCH_SLOTS, sub, d), jnp.float32),
        pltpu.VMEM((d, d), jnp.bfloat16),
        pltpu.VMEM((HALO, d), jnp.float32),
        pltpu.VMEM((SCRATCH_SLOTS, sub, d), jnp.bfloat16),
        pltpu.VMEM((SCRATCH_SLOTS, sub, d), jnp.bfloat16),
        pltpu.VMEM((SCRATCH_SLOTS, sub, d), jnp.bfloat16),
        pltpu.SMEM((1,), jnp.int32),
    ]
    in_hbm = pl.BlockSpec(memory_space=pl.ANY)
    in_vmem = pl.BlockSpec(memory_space=pltpu.VMEM)
    flat = [x.reshape(-1, d) for x in streams]
    ys = pl.pallas_call(
        functools.partial(_layers_kernel, seq_lens=tuple(x.shape[1] for x in streams), tm=tm),
        in_specs=[in_hbm] * len(streams) + [in_vmem] * len(params),
        out_specs=[in_hbm] * len(streams),
        out_shape=[jax.ShapeDtypeStruct(f.shape, f.dtype) for f in flat],
        scratch_shapes=scratch,
        compiler_params=pltpu.CompilerParams(vmem_limit_bytes=VMEM_LIMIT_BYTES),
    )(*flat, *params)
    return [y.reshape(x.shape) for y, x in zip(ys, streams)]


def kernel(x_prompt, x_sample, w_in, b_in, ln_v_g, ln_v_b, w_spatial, b_spatial, w_pool, b_pool, pool_scale,
           w_br_a, w_br_b, w_out, b_out, ln_g, ln_b):
    assert w_in.shape[0] == DEPTH
    row = lambda a: a[0].reshape(1, -1)
    b_sp = jnp.repeat(b_spatial[0].T, A_GROUP_W, axis=1)
    w_in_p, w_sp_p, w_pool_p, w_br_a_p, w_br_b_p, w_out_p = _pack_bf16([
        w_in[0], w_spatial[0].reshape(A_GROUPS * CHUNK, CHUNK),
        w_pool[0].reshape(len(POOL_WINDOWS) * POOL_GROUP_W, POOL_GROUP_W), w_br_a[0], w_br_b[0], w_out[0]])
    params = (
        w_in_p, row(b_in), row(ln_v_g), row(ln_v_b), w_sp_p.reshape(A_GROUPS, CHUNK // 2, CHUNK), b_sp,
        w_pool_p.reshape(len(POOL_WINDOWS), POOL_GROUP_W // 2, POOL_GROUP_W), row(b_pool), row(pool_scale),
        w_br_a_p, w_br_b_p, w_out_p, row(b_out), row(ln_g), row(ln_b), _pool_band_table(),
    )
    y_prompt, y_sample = _run_layers((x_prompt, x_sample), params)
    return y_prompt, y_sample
```

```python
import functools

import numpy as np
import jax
import jax.numpy as jnp
from jax.experimental import pallas as pl
from jax.experimental.pallas import tpu as pltpu

D_MODEL = 1024
CHUNK = 128
A_GROUPS = 8
A_GROUP_W = D_MODEL // A_GROUPS
POOL_WINDOWS = (2, 4, 8, 16)
POOL_GROUP_W = D_MODEL // len(POOL_WINDOWS)
DEPTH = 1
DEEPNORM_ALPHA = (2.0 * DEPTH) ** 0.25
LN_EPS = 1e-5

COL_U, COL_V, COL_ZA, COL_P, COL_ZB, COL_GA, COL_GB = (i * D_MODEL for i in range(7))

HALO = 16
BAND_K = CHUNK + 2 * HALO
TOKEN_TILE = 1024
SUB_TILE = 256
COL_BLOCK = 256
SCRATCH_SLOTS = 2
CAST_BLOCK_ROWS = 128
V7X_VMEM_BYTES = 64 * 1024 * 1024
VMEM_LIMIT_BYTES = V7X_VMEM_BYTES - 6 * 1024 * 1024


def _layernorm(x, g, b):
    mu = jnp.mean(x, axis=-1, keepdims=True)
    xc = x - mu
    var = jnp.mean(xc * xc, axis=-1, keepdims=True)
    return xc * jax.lax.rsqrt(var + LN_EPS) * g + b


def _silu(z):
    return z * jax.nn.sigmoid(z)


def _as_bf16(packed):
    return pltpu.bitcast(packed, jnp.bfloat16)


def _fold_pool_linear(w_in_ref, b_in_ref, w_pool_ref, wpp_ref, bpp_ref):
    for g in range(len(POOL_WINDOWS)):
        cols = slice(g * POOL_GROUP_W, (g + 1) * POOL_GROUP_W)
        pcols = slice(COL_P + g * POOL_GROUP_W, COL_P + (g + 1) * POOL_GROUP_W)
        w_pool_g = w_pool_ref[g * POOL_GROUP_W:(g + 1) * POOL_GROUP_W, :]
        wpp_ref[:, cols] = jnp.dot(w_in_ref[:, pcols], w_pool_g,
                                   preferred_element_type=jnp.float32).astype(jnp.bfloat16)
        b_p = jnp.broadcast_to(b_in_ref[:, pcols], bpp_ref[:, cols].shape).astype(jnp.bfloat16)
        bpp_ref[:, cols] = jnp.dot(b_p, w_pool_g, preferred_element_type=jnp.float32)


def _layer_step(tile, tiles_per_seq, xprev_ref, x_ref, xnext_ref, out_ref,
                w_in_ref, b_in_ref, ln_v_g_ref, ln_v_b_ref, w_sp_ref,
                b_sp_ref, w_pool_ref, b_pool_ref, pool_scale_ref, w_br_a_ref, w_br_b_ref, w_out_ref,
                b_out_ref, ln_g_ref, ln_b_ref, band_ref,
                xext_ref, vn_ref, vs_ref, pext_ref, m_ref, wpp_ref, bpp_ref, bra_ref, brb_ref, merged_ref):
    tm = x_ref.shape[0]
    sub = vn_ref.shape[1]
    n_sub = tm // sub
    n_chunks = sub // CHUNK
    j = tile % tiles_per_seq
    is_first = j == 0
    is_last = j == tiles_per_seq - 1

    def in_proj(lhs, col, width=D_MODEL):
        return (jnp.dot(lhs, w_in_ref[:, col:col + width], preferred_element_type=jnp.float32)
                + b_in_ref[:, col:col + width])

    def sub_tile(s):
        r0 = s * sub
        slot = s % vn_ref.shape[0]
        ext_rows = slice(0 if s == 0 else r0 + 2 * HALO, r0 + sub + 2 * HALO)
        if s == 0:
            xext_ref[0:HALO, :] = xprev_ref[...].astype(jnp.bfloat16)
        own = slice(max(ext_rows.start - HALO, 0), min(ext_rows.stop - HALO, tm))
        xext_ref[HALO + own.start:HALO + own.stop, :] = x_ref[own, :].astype(jnp.bfloat16)
        if s == n_sub - 1:
            xext_ref[HALO + tm:, :] = xnext_ref[...].astype(jnp.bfloat16)
        xb = xext_ref[HALO + r0:HALO + r0 + sub, :]

        v = in_proj(xb, COL_V)
        vn_ref[slot] = _layernorm(v, ln_v_g_ref[...], ln_v_b_ref[...]).astype(jnp.bfloat16)
        yield

        pext_ref[ext_rows, :] = (jnp.dot(xext_ref[ext_rows, :], wpp_ref[...], preferred_element_type=jnp.float32)
                                 + bpp_ref[0:1, :]).astype(jnp.bfloat16)
        for n in range(0, n_chunks, 2):
            rows0 = slice(n * CHUNK, (n + 1) * CHUNK)
            rows1 = slice((n + 1) * CHUNK, (n + 2) * CHUNK)
            for g in range(A_GROUPS):
                cols = slice(g * A_GROUP_W, (g + 1) * A_GROUP_W)
                rhs = jnp.concatenate([vn_ref[slot, rows0, cols], vn_ref[slot, rows1, cols]], axis=1)
                mixed = jnp.dot(w_sp_ref[g * CHUNK:(g + 1) * CHUNK, :], rhs, preferred_element_type=jnp.float32)
                vs_ref[slot, rows0, cols] = mixed[:, :A_GROUP_W] + b_sp_ref[:, cols]
                vs_ref[slot, rows1, cols] = mixed[:, A_GROUP_W:] + b_sp_ref[:, cols]
        for c in range(n_chunks):
            variant = jnp.where(jnp.logical_and(is_first, s == 0 and c == 0), 0,
                                jnp.where(jnp.logical_and(is_last, s == n_sub - 1 and c == n_chunks - 1), 2, 1))
            for g in range(len(POOL_WINDOWS)):
                cols = slice(g * POOL_GROUP_W, (g + 1) * POOL_GROUP_W)
                m_ref[slot, c * CHUNK:(c + 1) * CHUNK, cols] = (
                    jnp.dot(_as_bf16(band_ref[variant, g]), pext_ref[r0 + c * CHUNK:r0 + c * CHUNK + BAND_K, cols],
                            preferred_element_type=jnp.float32) + b_pool_ref[:, cols])

        for n in range(D_MODEL // COL_BLOCK):
            cols = slice(n * COL_BLOCK, (n + 1) * COL_BLOCK)
            z_a = in_proj(xb, COL_ZA + n * COL_BLOCK, COL_BLOCK)
            u = in_proj(xb, COL_U + n * COL_BLOCK, COL_BLOCK)
            bra_ref[slot, :, cols] = (u * vs_ref[slot, :, cols] * _silu(z_a)).astype(jnp.bfloat16)
            z_b = in_proj(xb, COL_ZB + n * COL_BLOCK, COL_BLOCK)
            brb_ref[slot, :, cols] = (m_ref[slot, :, cols] * pool_scale_ref[:, cols] * _silu(z_b)).astype(jnp.bfloat16)
        resid = DEEPNORM_ALPHA * x_ref[r0:r0 + sub, :] + b_out_ref[...]
        for n in range(D_MODEL // COL_BLOCK):
            cols = slice(n * COL_BLOCK, (n + 1) * COL_BLOCK)
            g_a = in_proj(xb, COL_GA + n * COL_BLOCK, COL_BLOCK)
            proj_a = jnp.dot(bra_ref[slot], w_br_a_ref[:, cols], preferred_element_type=jnp.float32)
            g_b = in_proj(xb, COL_GB + n * COL_BLOCK, COL_BLOCK)
            proj_b = jnp.dot(brb_ref[slot], w_br_b_ref[:, cols], preferred_element_type=jnp.float32)
            merged_ref[slot, :, cols] = (jax.nn.sigmoid(g_a) * proj_a
                                         + jax.nn.sigmoid(g_b) * proj_b).astype(jnp.bfloat16)
        yield

        out = jnp.dot(merged_ref[slot], w_out_ref[...], preferred_element_type=jnp.float32)
        out_ref[r0:r0 + sub, :] = _layernorm(resid + out, ln_g_ref[...], ln_b_ref[...])
        yield

    tiles = [sub_tile(s) for s in range(n_sub)]
    next(tiles[0])
    for s in range(n_sub):
        next(tiles[s])
        if s + 1 < n_sub:
            next(tiles[s + 1])
        next(tiles[s])


def _pool_band_table():
    t = np.arange(CHUNK)[:, None]
    k = np.arange(BAND_K)[None, :] - HALO
    table = np.zeros((3, len(POOL_WINDOWS), CHUNK, BAND_K), np.float32)
    for variant, valid in enumerate((k >= 0, np.ones_like(k, bool), k < CHUNK)):
        for g, w in enumerate(POOL_WINDOWS):
            in_window = (k >= t - w // 2) & (k < t + w - w // 2) & valid
            table[variant, g] = in_window / in_window.sum(axis=1, keepdims=True) - (k == t)
    bits = table.astype(jnp.bfloat16).view(np.uint16).astype(np.uint32)
    return jnp.asarray(bits[..., 0::2, :] | (bits[..., 1::2, :] << 16))


N_WEIGHTS = 6
N_SMALL = 10


def _layers_kernel(*refs, seq_lens, tm):
    n_streams = len(seq_lens)
    refs = list(refs)
    take = lambda n: [refs.pop(0) for _ in range(n)]
    xs, w_hbm, small, ys, wq = take(n_streams), take(N_WEIGHTS), take(N_SMALL), take(n_streams), take(N_WEIGHTS)
    step_ref = refs.pop()
    scratch = refs
    w_in, w_sp, w_pool, w_br_a, w_br_b, w_out = wq
    b_in, ln_v_g, ln_v_b, b_sp, b_pool, pool_scale, b_out, ln_g, ln_b, band = small
    params = (w_in, b_in, ln_v_g, ln_v_b, w_sp, b_sp, w_pool, b_pool, pool_scale, w_br_a, w_br_b, w_out,
              b_out, ln_g, ln_b, band)

    step_ref[0] = 0

    def cast_block(*blocks):
        i = step_ref[0]
        step_ref[0] = i + 1
        rows = pl.ds(pl.multiple_of(i * CAST_BLOCK_ROWS, CAST_BLOCK_ROWS), CAST_BLOCK_ROWS)
        for blk, dst in zip(blocks, wq):
            dst[rows, :] = blk[...].astype(jnp.bfloat16)

    pltpu.emit_pipeline(
        cast_block,
        grid=(D_MODEL // CAST_BLOCK_ROWS,),
        in_specs=[pl.BlockSpec((CAST_BLOCK_ROWS, w.shape[1]), lambda i: (i, 0)) for w in w_hbm],
        out_specs=[],
    )(*w_hbm)
    wpp_ref, bpp_ref = scratch[5], scratch[6]
    _fold_pool_linear(w_in, b_in, w_pool, wpp_ref, bpp_ref)

    d = D_MODEL
    halo_per_tile = tm // HALO
    for x2, y2, seq_len in zip(xs, ys, seq_lens):
        n_tok = x2.shape[0]
        last_halo_block = n_tok // HALO - 1
        step_ref[0] = 0

        def step(xprev_ref, x_ref, xnext_ref, out_ref, seq_len=seq_len):
            tile = step_ref[0]
            step_ref[0] = tile + 1
            _layer_step(tile, seq_len // tm, xprev_ref, x_ref, xnext_ref, out_ref, *params, *scratch)

        pltpu.emit_pipeline(
            step,
            grid=(n_tok // tm,),
            in_specs=[
                pl.BlockSpec((HALO, d), lambda i: (jnp.maximum(i * halo_per_tile - 1, 0), 0)),
                pl.BlockSpec((tm, d), lambda i: (i, 0)),
                pl.BlockSpec((HALO, d), lambda i, last=last_halo_block: (
                    jnp.minimum((i + 1) * halo_per_tile, last), 0)),
            ],
            out_specs=[pl.BlockSpec((tm, d), lambda i: (i, 0))],
        )(x2, x2, x2, y2)


def _run_layers(streams, weights, small, tm=TOKEN_TILE, sub=SUB_TILE):
    d = D_MODEL
    assert tm % sub == 0 and sub % (2 * CHUNK) == 0
    assert len(weights) == N_WEIGHTS and len(small) == N_SMALL
    assert all(w.ndim == 2 and w.shape[0] == d for w in weights)
    for x in streams:
        assert x.shape[2] == d and x.shape[1] % tm == 0
        assert x.shape[1] >= 2 * CHUNK
    scratch = [pltpu.VMEM(w.shape, jnp.bfloat16) for w in weights] + [
        pltpu.VMEM((tm + 2 * HALO, d), jnp.bfloat16),
        pltpu.VMEM((SCRATCH_SLOTS, sub, d), jnp.bfloat16),
        pltpu.VMEM((SCRATCH_SLOTS, sub, d), jnp.float32),
        pltpu.VMEM((tm + 2 * HALO, d), jnp.bfloat16),
        pltpu.VMEM((SCRATCH_SLOTS, sub, d), jnp.float32),
        pltpu.VMEM((d, d), jnp.bfloat16),
        pltpu.VMEM((HALO, d), jnp.float32),
        pltpu.VMEM((SCRATCH_SLOTS, sub, d), jnp.bfloat16),
        pltpu.VMEM((SCRATCH_SLOTS, sub, d), jnp.bfloat16),
        pltpu.VMEM((SCRATCH_SLOTS, sub, d), jnp.bfloat16),
        pltpu.SMEM((1,), jnp.int32),
    ]
    in_hbm = pl.BlockSpec(memory_space=pl.ANY)
    in_vmem = pl.BlockSpec(memory_space=pltpu.VMEM)
    flat = [x.reshape(-1, d) for x in streams]
    ys = pl.pallas_call(
        functools.partial(_layers_kernel, seq_lens=tuple(x.shape[1] for x in streams), tm=tm),
        in_specs=[in_hbm] * (len(streams) + N_WEIGHTS) + [in_vmem] * N_SMALL,
        out_specs=[in_hbm] * len(streams),
        out_shape=[jax.ShapeDtypeStruct(f.shape, f.dtype) for f in flat],
        scratch_shapes=scratch,
        compiler_params=pltpu.CompilerParams(vmem_limit_bytes=VMEM_LIMIT_BYTES),
    )(*flat, *weights, *small)
    return [y.reshape(x.shape) for y, x in zip(ys, streams)]


def kernel(x_prompt, x_sample, w_in, b_in, ln_v_g, ln_v_b, w_spatial, b_spatial, w_pool, b_pool, pool_scale,
           w_br_a, w_br_b, w_out, b_out, ln_g, ln_b):
    assert w_in.shape[0] == DEPTH
    row = lambda a: a[0].reshape(1, -1)
    b_sp = jnp.repeat(b_spatial[0].T, A_GROUP_W, axis=1)
    weights = (w_in[0], w_spatial[0].reshape(A_GROUPS * CHUNK, CHUNK),
               w_pool[0].reshape(len(POOL_WINDOWS) * POOL_GROUP_W, POOL_GROUP_W), w_br_a[0], w_br_b[0], w_out[0])
    small = (row(b_in), row(ln_v_g), row(ln_v_b), b_sp, row(b_pool), row(pool_scale), row(b_out), row(ln_g),
             row(ln_b), _pool_band_table())
    y_prompt, y_sample = _run_layers((x_prompt, x_sample), weights, small)
    return y_prompt, y_sample
```

```python
import functools

import numpy as np
import jax
import jax.numpy as jnp
from jax.experimental import pallas as pl
from jax.experimental.pallas import tpu as pltpu

D_MODEL = 1024
CHUNK = 128
A_GROUPS = 8
A_GROUP_W = D_MODEL // A_GROUPS
POOL_WINDOWS = (2, 4, 8, 16)
POOL_GROUP_W = D_MODEL // len(POOL_WINDOWS)
DEPTH = 1
DEEPNORM_ALPHA = (2.0 * DEPTH) ** 0.25
LN_EPS = 1e-5

COL_U, COL_V, COL_ZA, COL_P, COL_ZB, COL_GA, COL_GB = (i * D_MODEL for i in range(7))

HALO = 16
BAND_K = CHUNK + 2 * HALO
TOKEN_TILE = 1024
SUB_TILE = 256
COL_BLOCK = 256
SCRATCH_SLOTS = 2
CAST_BLOCK_ROWS = 128
V7X_VMEM_BYTES = 64 * 1024 * 1024
VMEM_LIMIT_BYTES = V7X_VMEM_BYTES - 6 * 1024 * 1024


def _layernorm(x, g, b):
    mu = jnp.mean(x, axis=-1, keepdims=True)
    xc = x - mu
    var = jnp.mean(xc * xc, axis=-1, keepdims=True)
    return xc * jax.lax.rsqrt(var + LN_EPS) * g + b


def _silu(z):
    return z * jax.nn.sigmoid(z)


def _as_bf16(packed):
    return pltpu.bitcast(packed, jnp.bfloat16)


def _fold_pool_linear(w_in_ref, b_in_ref, w_pool_ref, wpp_ref, bpp_ref):
    for g in range(len(POOL_WINDOWS)):
        cols = slice(g * POOL_GROUP_W, (g + 1) * POOL_GROUP_W)
        pcols = slice(COL_P + g * POOL_GROUP_W, COL_P + (g + 1) * POOL_GROUP_W)
        w_pool_g = _as_bf16(w_pool_ref[g * POOL_GROUP_W // 2:(g + 1) * POOL_GROUP_W // 2, :])
        wpp_ref[:, cols] = jnp.dot(_as_bf16(w_in_ref[:, pcols]), w_pool_g,
                                   preferred_element_type=jnp.float32).astype(jnp.bfloat16)
        b_p = jnp.broadcast_to(b_in_ref[:, pcols], bpp_ref[:, cols].shape).astype(jnp.bfloat16)
        bpp_ref[:, cols] = jnp.dot(b_p, w_pool_g, preferred_element_type=jnp.float32)


def _layer_step(tile, tiles_per_seq, xprev_ref, x_ref, xnext_ref, out_ref,
                w_in_ref, b_in_ref, ln_v_g_ref, ln_v_b_ref, w_sp_ref,
                b_sp_ref, w_pool_ref, b_pool_ref, pool_scale_ref, w_br_a_ref, w_br_b_ref, w_out_ref,
                b_out_ref, ln_g_ref, ln_b_ref, band_ref,
                xext_ref, vn_ref, vs_ref, pext_ref, m_ref, wpp_ref, bpp_ref, bra_ref, brb_ref, merged_ref):
    tm = x_ref.shape[0]
    sub = vn_ref.shape[1]
    n_sub = tm // sub
    n_chunks = sub // CHUNK
    j = tile % tiles_per_seq
    is_first = j == 0
    is_last = j == tiles_per_seq - 1

    def in_proj(lhs, col, width=D_MODEL):
        return (jnp.dot(lhs, _as_bf16(w_in_ref[:, col:col + width]), preferred_element_type=jnp.float32)
                + b_in_ref[:, col:col + width])

    def sub_tile(s):
        r0 = s * sub
        slot = s % vn_ref.shape[0]
        ext_rows = slice(0 if s == 0 else r0 + 2 * HALO, r0 + sub + 2 * HALO)
        if s == 0:
            xext_ref[0:HALO, :] = xprev_ref[...].astype(jnp.bfloat16)
        own = slice(max(ext_rows.start - HALO, 0), min(ext_rows.stop - HALO, tm))
        xext_ref[HALO + own.start:HALO + own.stop, :] = x_ref[own, :].astype(jnp.bfloat16)
        if s == n_sub - 1:
            xext_ref[HALO + tm:, :] = xnext_ref[...].astype(jnp.bfloat16)
        xb = xext_ref[HALO + r0:HALO + r0 + sub, :]

        v = in_proj(xb, COL_V)
        vn_ref[slot] = _layernorm(v, ln_v_g_ref[...], ln_v_b_ref[...]).astype(jnp.bfloat16)
        yield

        pext_ref[ext_rows, :] = (jnp.dot(xext_ref[ext_rows, :], wpp_ref[...], preferred_element_type=jnp.float32)
                                 + bpp_ref[0:1, :]).astype(jnp.bfloat16)
        for n in range(0, n_chunks, 2):
            rows0 = slice(n * CHUNK, (n + 1) * CHUNK)
            rows1 = slice((n + 1) * CHUNK, (n + 2) * CHUNK)
            for g in range(A_GROUPS):
                cols = slice(g * A_GROUP_W, (g + 1) * A_GROUP_W)
                rhs = jnp.concatenate([vn_ref[slot, rows0, cols], vn_ref[slot, rows1, cols]], axis=1)
                mixed = jnp.dot(_as_bf16(w_sp_ref[g * CHUNK // 2:(g + 1) * CHUNK // 2, :]), rhs,
                                preferred_element_type=jnp.float32)
                vs_ref[slot, rows0, cols] = mixed[:, :A_GROUP_W] + b_sp_ref[:, cols]
                vs_ref[slot, rows1, cols] = mixed[:, A_GROUP_W:] + b_sp_ref[:, cols]
        for c in range(n_chunks):
            variant = jnp.where(jnp.logical_and(is_first, s == 0 and c == 0), 0,
                                jnp.where(jnp.logical_and(is_last, s == n_sub - 1 and c == n_chunks - 1), 2, 1))
            for g in range(len(POOL_WINDOWS)):
                cols = slice(g * POOL_GROUP_W, (g + 1) * POOL_GROUP_W)
                m_ref[slot, c * CHUNK:(c + 1) * CHUNK, cols] = (
                    jnp.dot(_as_bf16(band_ref[variant, g]), pext_ref[r0 + c * CHUNK:r0 + c * CHUNK + BAND_K, cols],
                            preferred_element_type=jnp.float32) + b_pool_ref[:, cols])

        for n in range(D_MODEL // COL_BLOCK):
            cols = slice(n * COL_BLOCK, (n + 1) * COL_BLOCK)
            z_a = in_proj(xb, COL_ZA + n * COL_BLOCK, COL_BLOCK)
            u = in_proj(xb, COL_U + n * COL_BLOCK, COL_BLOCK)
            bra_ref[slot, :, cols] = (u * vs_ref[slot, :, cols] * _silu(z_a)).astype(jnp.bfloat16)
            z_b = in_proj(xb, COL_ZB + n * COL_BLOCK, COL_BLOCK)
            brb_ref[slot, :, cols] = (m_ref[slot, :, cols] * pool_scale_ref[:, cols] * _silu(z_b)).astype(jnp.bfloat16)
        resid = DEEPNORM_ALPHA * x_ref[r0:r0 + sub, :] + b_out_ref[...]
        for n in range(D_MODEL // COL_BLOCK):
            cols = slice(n * COL_BLOCK, (n + 1) * COL_BLOCK)
            g_a = in_proj(xb, COL_GA + n * COL_BLOCK, COL_BLOCK)
            proj_a = jnp.dot(bra_ref[slot], _as_bf16(w_br_a_ref[:, cols]), preferred_element_type=jnp.float32)
            g_b = in_proj(xb, COL_GB + n * COL_BLOCK, COL_BLOCK)
            proj_b = jnp.dot(brb_ref[slot], _as_bf16(w_br_b_ref[:, cols]), preferred_element_type=jnp.float32)
            merged_ref[slot, :, cols] = (jax.nn.sigmoid(g_a) * proj_a
                                         + jax.nn.sigmoid(g_b) * proj_b).astype(jnp.bfloat16)
        yield

        out = jnp.dot(merged_ref[slot], _as_bf16(w_out_ref[...]), preferred_element_type=jnp.float32)
        out_ref[r0:r0 + sub, :] = _layernorm(resid + out, ln_g_ref[...], ln_b_ref[...])
        yield

    tiles = [sub_tile(s) for s in range(n_sub)]
    next(tiles[0])
    for s in range(n_sub):
        next(tiles[s])
        if s + 1 < n_sub:
            next(tiles[s + 1])
        next(tiles[s])


def _pool_band_table():
    t = np.arange(CHUNK)[:, None]
    k = np.arange(BAND_K)[None, :] - HALO
    table = np.zeros((3, len(POOL_WINDOWS), CHUNK, BAND_K), np.float32)
    for variant, valid in enumerate((k >= 0, np.ones_like(k, bool), k < CHUNK)):
        for g, w in enumerate(POOL_WINDOWS):
            in_window = (k >= t - w // 2) & (k < t + w - w // 2) & valid
            table[variant, g] = in_window / in_window.sum(axis=1, keepdims=True) - (k == t)
    bits = table.astype(jnp.bfloat16).view(np.uint16).astype(np.uint32)
    return jnp.asarray(bits[..., 0::2, :] | (bits[..., 1::2, :] << 16))


N_WEIGHTS = 6
N_SMALL = 10


def _layers_kernel(*refs, seq_lens, tm):
    n_streams = len(seq_lens)
    refs = list(refs)
    take = lambda n: [refs.pop(0) for _ in range(n)]
    xs, w_hbm, small, ys, wq = take(n_streams), take(N_WEIGHTS), take(N_SMALL), take(n_streams), take(N_WEIGHTS)
    step_ref = refs.pop()
    scratch = refs
    w_in, w_sp, w_pool, w_br_a, w_br_b, w_out = wq
    b_in, ln_v_g, ln_v_b, b_sp, b_pool, pool_scale, b_out, ln_g, ln_b, band = small
    params = (w_in, b_in, ln_v_g, ln_v_b, w_sp, b_sp, w_pool, b_pool, pool_scale, w_br_a, w_br_b, w_out,
              b_out, ln_g, ln_b, band)

    step_ref[0] = 0

    def cast_block(*blocks):
        i = step_ref[0]
        step_ref[0] = i + 1
        packed_rows = CAST_BLOCK_ROWS // 2
        rows = pl.ds(pl.multiple_of(i * packed_rows, packed_rows), packed_rows)
        for blk, dst in zip(blocks, wq):
            dst[rows, :] = pltpu.bitcast(blk[...].astype(jnp.bfloat16), jnp.uint32)

    pltpu.emit_pipeline(
        cast_block,
        grid=(D_MODEL // CAST_BLOCK_ROWS,),
        in_specs=[pl.BlockSpec((CAST_BLOCK_ROWS, w.shape[1]), lambda i: (i, 0)) for w in w_hbm],
        out_specs=[],
    )(*w_hbm)
    wpp_ref, bpp_ref = scratch[5], scratch[6]
    _fold_pool_linear(w_in, b_in, w_pool, wpp_ref, bpp_ref)

    d = D_MODEL
    halo_per_tile = tm // HALO
    for x2, y2, seq_len in zip(xs, ys, seq_lens):
        n_tok = x2.shape[0]
        last_halo_block = n_tok // HALO - 1
        step_ref[0] = 0

        def step(xprev_ref, x_ref, xnext_ref, out_ref, seq_len=seq_len):
            tile = step_ref[0]
            step_ref[0] = tile + 1
            _layer_step(tile, seq_len // tm, xprev_ref, x_ref, xnext_ref, out_ref, *params, *scratch)

        pltpu.emit_pipeline(
            step,
            grid=(n_tok // tm,),
            in_specs=[
                pl.BlockSpec((HALO, d), lambda i: (jnp.maximum(i * halo_per_tile - 1, 0), 0)),
                pl.BlockSpec((tm, d), lambda i: (i, 0)),
                pl.BlockSpec((HALO, d), lambda i, last=last_halo_block: (
                    jnp.minimum((i + 1) * halo_per_tile, last), 0)),
            ],
            out_specs=[pl.BlockSpec((tm, d), lambda i: (i, 0))],
        )(x2, x2, x2, y2)


def _run_layers(streams, weights, small, tm=TOKEN_TILE, sub=SUB_TILE):
    d = D_MODEL
    assert tm % sub == 0 and sub % (2 * CHUNK) == 0
    assert len(weights) == N_WEIGHTS and len(small) == N_SMALL
    assert all(w.ndim == 2 and w.shape[0] == d for w in weights)
    for x in streams:
        assert x.shape[2] == d and x.shape[1] % tm == 0
        assert x.shape[1] >= 2 * CHUNK
    scratch = [pltpu.VMEM((w.shape[0] // 2, w.shape[1]), jnp.uint32) for w in weights] + [
        pltpu.VMEM((tm + 2 * HALO, d), jnp.bfloat16),
        pltpu.VMEM((SCRATCH_SLOTS, sub, d), jnp.bfloat16),
        pltpu.VMEM((SCRATCH_SLOTS, sub, d), jnp.float32),
        pltpu.VMEM((tm + 2 * HALO, d), jnp.bfloat16),
        pltpu.VMEM((SCRATCH_SLOTS, sub, d), jnp.float32),
        pltpu.VMEM((d, d), jnp.bfloat16),
        pltpu.VMEM((HALO, d), jnp.float32),
        pltpu.VMEM((SCRATCH_SLOTS, sub, d), jnp.bfloat16),
        pltpu.VMEM((SCRATCH_SLOTS, sub, d), jnp.bfloat16),
        pltpu.VMEM((SCRATCH_SLOTS, sub, d), jnp.bfloat16),
        pltpu.SMEM((1,), jnp.int32),
    ]
    in_hbm = pl.BlockSpec(memory_space=pl.ANY)
    in_vmem = pl.BlockSpec(memory_space=pltpu.VMEM)
    flat = [x.reshape(-1, d) for x in streams]
    ys = pl.pallas_call(
        functools.partial(_layers_kernel, seq_lens=tuple(x.shape[1] for x in streams), tm=tm),
        in_specs=[in_hbm] * (len(streams) + N_WEIGHTS) + [in_vmem] * N_SMALL,
        out_specs=[in_hbm] * len(streams),
        out_shape=[jax.ShapeDtypeStruct(f.shape, f.dtype) for f in flat],
        scratch_shapes=scratch,
        compiler_params=pltpu.CompilerParams(vmem_limit_bytes=VMEM_LIMIT_BYTES),
    )(*flat, *weights, *small)
    return [y.reshape(x.shape) for y, x in zip(ys, streams)]


def kernel(x_prompt, x_sample, w_in, b_in, ln_v_g, ln_v_b, w_spatial, b_spatial, w_pool, b_pool, pool_scale,
           w_br_a, w_br_b, w_out, b_out, ln_g, ln_b):
    assert w_in.shape[0] == DEPTH
    row = lambda a: a[0].reshape(1, -1)
    b_sp = jnp.repeat(b_spatial[0].T, A_GROUP_W, axis=1)
    weights = (w_in[0], w_spatial[0].reshape(A_GROUPS * CHUNK, CHUNK),
               w_pool[0].reshape(len(POOL_WINDOWS) * POOL_GROUP_W, POOL_GROUP_W), w_br_a[0], w_br_b[0], w_out[0])
    small = (row(b_in), row(ln_v_g), row(ln_v_b), b_sp, row(b_pool), row(pool_scale), row(b_out), row(ln_g),
             row(ln_b), _pool_band_table())
    y_prompt, y_sample = _run_layers((x_prompt, x_sample), weights, small)
    return y_prompt, y_sample
```

```python
import functools

import numpy as np
import jax
import jax.numpy as jnp
from jax.experimental import pallas as pl
from jax.experimental.pallas import tpu as pltpu

D_MODEL = 1024
CHUNK = 128
A_GROUPS = 8
A_GROUP_W = D_MODEL // A_GROUPS
POOL_WINDOWS = (2, 4, 8, 16)
POOL_GROUP_W = D_MODEL // len(POOL_WINDOWS)
DEPTH = 1
DEEPNORM_ALPHA = (2.0 * DEPTH) ** 0.25
LN_EPS = 1e-5

COL_U, COL_V, COL_ZA, COL_P, COL_ZB, COL_GA, COL_GB = (i * D_MODEL for i in range(7))

HALO = 16
BAND_K = CHUNK + 2 * HALO
TOKEN_TILE = 1024
SUB_TILE = 256
COL_BLOCK = 256
SCRATCH_SLOTS = 2
CAST_BLOCK_ROWS = 128
V7X_VMEM_BYTES = 64 * 1024 * 1024
VMEM_LIMIT_BYTES = V7X_VMEM_BYTES - 6 * 1024 * 1024


def _layernorm(x, g, b):
    mu = jnp.mean(x, axis=-1, keepdims=True)
    xc = x - mu
    var = jnp.mean(xc * xc, axis=-1, keepdims=True)
    return xc * jax.lax.rsqrt(var + LN_EPS) * g + b


def _silu(z):
    return z * jax.nn.sigmoid(z)


def _as_bf16(packed):
    return pltpu.bitcast(packed, jnp.bfloat16)


def _fold_pool_linear(w_in_ref, b_in_ref, w_pool_ref, wpp_ref, bpp_ref):
    for g in range(len(POOL_WINDOWS)):
        cols = slice(g * POOL_GROUP_W, (g + 1) * POOL_GROUP_W)
        pcols = slice(COL_P + g * POOL_GROUP_W, COL_P + (g + 1) * POOL_GROUP_W)
        w_pool_g = _as_bf16(w_pool_ref[g * POOL_GROUP_W // 2:(g + 1) * POOL_GROUP_W // 2, :])
        wpp_ref[:, cols] = jnp.dot(_as_bf16(w_in_ref[:, pcols]), w_pool_g,
                                   preferred_element_type=jnp.float32).astype(jnp.bfloat16)
        b_p = jnp.broadcast_to(b_in_ref[:, pcols], bpp_ref[:, cols].shape).astype(jnp.bfloat16)
        bpp_ref[:, cols] = jnp.dot(b_p, w_pool_g, preferred_element_type=jnp.float32)


def _layer_step(tile, tiles_per_seq, xprev_ref, x_ref, xnext_ref, out_ref,
                w_in_ref, b_in_ref, ln_v_g_ref, ln_v_b_ref, w_sp_ref,
                b_sp_ref, w_pool_ref, b_pool_ref, pool_scale_ref, w_br_a_ref, w_br_b_ref, w_out_ref,
                b_out_ref, ln_g_ref, ln_b_ref, band_ref,
                xext_ref, vn_ref, vs_ref, pext_ref, m_ref, wpp_ref, bpp_ref, bra_ref, brb_ref, merged_ref):
    tm = x_ref.shape[0]
    sub = vn_ref.shape[1]
    n_sub = tm // sub
    n_chunks = sub // CHUNK
    j = tile % tiles_per_seq
    is_first = j == 0
    is_last = j == tiles_per_seq - 1

    def in_proj(lhs, col, width=D_MODEL):
        return (jnp.dot(lhs, _as_bf16(w_in_ref[:, col:col + width]), preferred_element_type=jnp.float32)
                + b_in_ref[:, col:col + width])

    def sub_tile(s):
        r0 = s * sub
        slot = s % vn_ref.shape[0]
        ext_rows = slice(0 if s == 0 else r0 + 2 * HALO, r0 + sub + 2 * HALO)
        if s == 0:
            xext_ref[0:HALO, :] = xprev_ref[...].astype(jnp.bfloat16)
        own = slice(max(ext_rows.start - HALO, 0), min(ext_rows.stop - HALO, tm))
        xext_ref[HALO + own.start:HALO + own.stop, :] = x_ref[own, :].astype(jnp.bfloat16)
        if s == n_sub - 1:
            xext_ref[HALO + tm:, :] = xnext_ref[...].astype(jnp.bfloat16)
        xb = xext_ref[HALO + r0:HALO + r0 + sub, :]

        v = in_proj(xb, COL_V)
        vn_ref[slot] = _layernorm(v, ln_v_g_ref[...], ln_v_b_ref[...]).astype(jnp.bfloat16)
        yield

        pext_ref[ext_rows, :] = (jnp.dot(xext_ref[ext_rows, :], wpp_ref[...], preferred_element_type=jnp.float32)
                                 + bpp_ref[0:1, :]).astype(jnp.bfloat16)
        for n in range(0, n_chunks, 2):
            rows0 = slice(n * CHUNK, (n + 1) * CHUNK)
            rows1 = slice((n + 1) * CHUNK, (n + 2) * CHUNK)
            for g in range(A_GROUPS):
                cols = slice(g * A_GROUP_W, (g + 1) * A_GROUP_W)
                rhs = jnp.concatenate([vn_ref[slot, rows0, cols], vn_ref[slot, rows1, cols]], axis=1)
                mixed = jnp.dot(_as_bf16(w_sp_ref[g * CHUNK // 2:(g + 1) * CHUNK // 2, :]), rhs,
                                preferred_element_type=jnp.float32)
                vs_ref[slot, rows0, cols] = mixed[:, :A_GROUP_W] + b_sp_ref[:, cols]
                vs_ref[slot, rows1, cols] = mixed[:, A_GROUP_W:] + b_sp_ref[:, cols]
        for c in range(n_chunks):
            variant = jnp.where(jnp.logical_and(is_first, s == 0 and c == 0), 0,
                                jnp.where(jnp.logical_and(is_last, s == n_sub - 1 and c == n_chunks - 1), 2, 1))
            for g in range(len(POOL_WINDOWS)):
                cols = slice(g * POOL_GROUP_W, (g + 1) * POOL_GROUP_W)
                m_ref[slot, c * CHUNK:(c + 1) * CHUNK, cols] = (
                    jnp.dot(_as_bf16(band_ref[variant, g]), pext_ref[r0 + c * CHUNK:r0 + c * CHUNK + BAND_K, cols],
                            preferred_element_type=jnp.float32) + b_pool_ref[:, cols])

        for n in range(D_MODEL // COL_BLOCK):
            cols = slice(n * COL_BLOCK, (n + 1) * COL_BLOCK)
            z_a = in_proj(xb, COL_ZA + n * COL_BLOCK, COL_BLOCK)
            u = in_proj(xb, COL_U + n * COL_BLOCK, COL_BLOCK)
            bra_ref[slot, :, cols] = (u * vs_ref[slot, :, cols] * _silu(z_a)).astype(jnp.bfloat16)
            z_b = in_proj(xb, COL_ZB + n * COL_BLOCK, COL_BLOCK)
            brb_ref[slot, :, cols] = (m_ref[slot, :, cols] * pool_scale_ref[:, cols] * _silu(z_b)).astype(jnp.bfloat16)
        resid = DEEPNORM_ALPHA * x_ref[r0:r0 + sub, :] + b_out_ref[...]
        for n in range(D_MODEL // COL_BLOCK):
            cols = slice(n * COL_BLOCK, (n + 1) * COL_BLOCK)
            g_a = in_proj(xb, COL_GA + n * COL_BLOCK, COL_BLOCK)
            proj_a = jnp.dot(bra_ref[slot], _as_bf16(w_br_a_ref[:, cols]), preferred_element_type=jnp.float32)
            g_b = in_proj(xb, COL_GB + n * COL_BLOCK, COL_BLOCK)
            proj_b = jnp.dot(brb_ref[slot], _as_bf16(w_br_b_ref[:, cols]), preferred_element_type=jnp.float32)
            merged_ref[slot, :, cols] = (jax.nn.sigmoid(g_a) * proj_a
                                         + jax.nn.sigmoid(g_b) * proj_b).astype(jnp.bfloat16)
        yield

        out = jnp.dot(merged_ref[slot], _as_bf16(w_out_ref[...]), preferred_element_type=jnp.float32)
        out_ref[r0:r0 + sub, :] = _layernorm(resid + out, ln_g_ref[...], ln_b_ref[...])
        yield

    tiles = [sub_tile(s) for s in range(n_sub)]
    next(tiles[0])
    for s in range(n_sub):
        next(tiles[s])
        if s + 1 < n_sub:
            next(tiles[s + 1])
        next(tiles[s])


def _pool_band_table():
    t = np.arange(CHUNK)[:, None]
    k = np.arange(BAND_K)[None, :] - HALO
    table = np.zeros((3, len(POOL_WINDOWS), CHUNK, BAND_K), np.float32)
    for variant, valid in enumerate((k >= 0, np.ones_like(k, bool), k < CHUNK)):
        for g, w in enumerate(POOL_WINDOWS):
            in_window = (k >= t - w // 2) & (k < t + w - w // 2) & valid
            table[variant, g] = in_window / in_window.sum(axis=1, keepdims=True) - (k == t)
    bits = table.astype(jnp.bfloat16).view(np.uint16).astype(np.uint32)
    return jnp.asarray(bits[..., 0::2, :] | (bits[..., 1::2, :] << 16))


N_WEIGHTS = 6
N_SMALL = 10


def _layers_kernel(*refs, seq_lens, tm):
    n_streams = len(seq_lens)
    refs = list(refs)
    take = lambda n: [refs.pop(0) for _ in range(n)]
    xs, w_hbm, small, ys, wq = take(n_streams), take(N_WEIGHTS), take(N_SMALL), take(n_streams), take(N_WEIGHTS)
    step_ref = refs.pop()
    scratch = refs
    w_in, w_sp, w_pool, w_br_a, w_br_b, w_out = wq
    b_in, ln_v_g, ln_v_b, b_sp, b_pool, pool_scale, b_out, ln_g, ln_b, band = small
    params = (w_in, b_in, ln_v_g, ln_v_b, w_sp, b_sp, w_pool, b_pool, pool_scale, w_br_a, w_br_b, w_out,
              b_out, ln_g, ln_b, band)

    step_ref[0] = 0

    def cast_block(*blocks):
        i = step_ref[0]
        step_ref[0] = i + 1
        packed_rows = CAST_BLOCK_ROWS // 2
        rows = pl.ds(pl.multiple_of(i * packed_rows, packed_rows), packed_rows)
        for blk, dst in zip(blocks, wq):
            dst[rows, :] = pltpu.bitcast(blk[...].astype(jnp.bfloat16), jnp.uint32)

    pltpu.emit_pipeline(
        cast_block,
        grid=(D_MODEL // CAST_BLOCK_ROWS,),
        in_specs=[pl.BlockSpec((CAST_BLOCK_ROWS, w.shape[1]), lambda i: (i, 0)) for w in w_hbm],
        out_specs=[],
        trace_scopes=False,
    )(*w_hbm)
    wpp_ref, bpp_ref = scratch[5], scratch[6]
    _fold_pool_linear(w_in, b_in, w_pool, wpp_ref, bpp_ref)

    d = D_MODEL
    halo_per_tile = tm // HALO
    for x2, y2, seq_len in zip(xs, ys, seq_lens):
        n_tok = x2.shape[0]
        last_halo_block = n_tok // HALO - 1
        step_ref[0] = 0

        def step(xprev_ref, x_ref, xnext_ref, out_ref, seq_len=seq_len):
            tile = step_ref[0]
            step_ref[0] = tile + 1
            _layer_step(tile, seq_len // tm, xprev_ref, x_ref, xnext_ref, out_ref, *params, *scratch)

        pltpu.emit_pipeline(
            step,
            grid=(n_tok // tm,),
            in_specs=[
                pl.BlockSpec((HALO, d), lambda i: (jnp.maximum(i * halo_per_tile - 1, 0), 0)),
                pl.BlockSpec((tm, d), lambda i: (i, 0)),
                pl.BlockSpec((HALO, d), lambda i, last=last_halo_block: (
                    jnp.minimum((i + 1) * halo_per_tile, last), 0)),
            ],
            out_specs=[pl.BlockSpec((tm, d), lambda i: (i, 0))],
            trace_scopes=False,
        )(x2, x2, x2, y2)


def _run_layers(streams, weights, small, tm=TOKEN_TILE, sub=SUB_TILE):
    d = D_MODEL
    assert tm % sub == 0 and sub % (2 * CHUNK) == 0
    assert len(weights) == N_WEIGHTS and len(small) == N_SMALL
    assert all(w.ndim == 2 and w.shape[0] == d for w in weights)
    for x in streams:
        assert x.shape[2] == d and x.shape[1] % tm == 0
        assert x.shape[1] >= 2 * CHUNK
    scratch = [pltpu.VMEM((w.shape[0] // 2, w.shape[1]), jnp.uint32) for w in weights] + [
        pltpu.VMEM((tm + 2 * HALO, d), jnp.bfloat16),
        pltpu.VMEM((SCRATCH_SLOTS, sub, d), jnp.bfloat16),
        pltpu.VMEM((SCRATCH_SLOTS, sub, d), jnp.float32),
        pltpu.VMEM((tm + 2 * HALO, d), jnp.bfloat16),
        pltpu.VMEM((SCRATCH_SLOTS, sub, d), jnp.float32),
        pltpu.VMEM((d, d), jnp.bfloat16),
        pltpu.VMEM((HALO, d), jnp.float32),
        pltpu.VMEM((SCRATCH_SLOTS, sub, d), jnp.bfloat16),
        pltpu.VMEM((SCRATCH_SLOTS, sub, d), jnp.bfloat16),
        pltpu.VMEM((SCRATCH_SLOTS, sub, d), jnp.bfloat16),
        pltpu.SMEM((1,), jnp.int32),
    ]
    in_hbm = pl.BlockSpec(memory_space=pl.ANY)
    in_vmem = pl.BlockSpec(memory_space=pltpu.VMEM)
    flat = [x.reshape(-1, d) for x in streams]
    ys = pl.pallas_call(
        functools.partial(_layers_kernel, seq_lens=tuple(x.shape[1] for x in streams), tm=tm),
        in_specs=[in_hbm] * (len(streams) + N_WEIGHTS) + [in_vmem] * N_SMALL,
        out_specs=[in_hbm] * len(streams),
        out_shape=[jax.ShapeDtypeStruct(f.shape, f.dtype) for f in flat],
        scratch_shapes=scratch,
        compiler_params=pltpu.CompilerParams(vmem_limit_bytes=VMEM_LIMIT_BYTES),
    )(*flat, *weights, *small)
    return [y.reshape(x.shape) for y, x in zip(ys, streams)]


def kernel(x_prompt, x_sample, w_in, b_in, ln_v_g, ln_v_b, w_spatial, b_spatial, w_pool, b_pool, pool_scale,
           w_br_a, w_br_b, w_out, b_out, ln_g, ln_b):
    assert w_in.shape[0] == DEPTH
    row = lambda a: a[0].reshape(1, -1)
    b_sp = jnp.repeat(b_spatial[0].T, A_GROUP_W, axis=1)
    weights = (w_in[0], w_spatial[0].reshape(A_GROUPS * CHUNK, CHUNK),
               w_pool[0].reshape(len(POOL_WINDOWS) * POOL_GROUP_W, POOL_GROUP_W), w_br_a[0], w_br_b[0], w_out[0])
    small = (row(b_in), row(ln_v_g), row(ln_v_b), b_sp, row(b_pool), row(pool_scale), row(b_out), row(ln_g),
             row(ln_b), _pool_band_table())
    y_prompt, y_sample = _run_layers((x_prompt, x_sample), weights, small)
    return y_prompt, y_sample
```

```python
import functools

import numpy as np
import jax
import jax.numpy as jnp
from jax.experimental import pallas as pl
from jax.experimental.pallas import tpu as pltpu

D_MODEL = 1024
CHUNK = 128
A_GROUPS = 8
A_GROUP_W = D_MODEL // A_GROUPS
POOL_WINDOWS = (2, 4, 8, 16)
POOL_GROUP_W = D_MODEL // len(POOL_WINDOWS)
DEPTH = 1
DEEPNORM_ALPHA = (2.0 * DEPTH) ** 0.25
LN_EPS = 1e-5

COL_U, COL_V, COL_ZA, COL_P, COL_ZB, COL_GA, COL_GB = (i * D_MODEL for i in range(7))

HALO = 16
BAND_K = CHUNK + 2 * HALO
TOKEN_TILE = 1024
SUB_TILE = 256
COL_BLOCK = 256
SCRATCH_SLOTS = 2
CAST_BLOCK_ROWS = 128
V7X_VMEM_BYTES = 64 * 1024 * 1024
VMEM_LIMIT_BYTES = V7X_VMEM_BYTES - 6 * 1024 * 1024


def _layernorm(x, g, b):
    mu = jnp.mean(x, axis=-1, keepdims=True)
    xc = x - mu
    var = jnp.mean(xc * xc, axis=-1, keepdims=True)
    return xc * jax.lax.rsqrt(var + LN_EPS) * g + b


def _silu(z):
    return z * jax.nn.sigmoid(z)


def _as_bf16(packed):
    return pltpu.bitcast(packed, jnp.bfloat16)


def _fold_pool_linear(w_in_ref, b_in_ref, w_pool_ref, wpp_ref, bpp_ref):
    for g in range(len(POOL_WINDOWS)):
        cols = slice(g * POOL_GROUP_W, (g + 1) * POOL_GROUP_W)
        pcols = slice(COL_P + g * POOL_GROUP_W, COL_P + (g + 1) * POOL_GROUP_W)
        w_pool_g = _as_bf16(w_pool_ref[g * POOL_GROUP_W // 2:(g + 1) * POOL_GROUP_W // 2, :])
        wpp_ref[:, cols] = jnp.dot(_as_bf16(w_in_ref[:, pcols]), w_pool_g,
                                   preferred_element_type=jnp.float32).astype(jnp.bfloat16)
        b_p = jnp.broadcast_to(b_in_ref[:, pcols], bpp_ref[:, cols].shape).astype(jnp.bfloat16)
        bpp_ref[:, cols] = jnp.dot(b_p, w_pool_g, preferred_element_type=jnp.float32)


def _layer_step(tile, tiles_per_seq, xprev_ref, x_ref, xnext_ref, out_ref,
                w_in_ref, b_in_ref, ln_v_g_ref, ln_v_b_ref, w_sp_ref,
                b_sp_ref, w_pool_ref, b_pool_ref, pool_scale_ref, w_br_a_ref, w_br_b_ref, w_out_ref,
                b_out_ref, ln_g_ref, ln_b_ref, band_ref,
                xext_ref, vn_ref, vs_ref, pext_ref, m_ref, wpp_ref, bpp_ref, bra_ref, brb_ref, merged_ref):
    tm = x_ref.shape[0]
    sub = vn_ref.shape[1]
    n_sub = tm // sub
    n_chunks = sub // CHUNK
    j = tile % tiles_per_seq
    is_first = j == 0
    is_last = j == tiles_per_seq - 1

    def in_proj(lhs, col, width=D_MODEL):
        return (jnp.dot(lhs, _as_bf16(w_in_ref[:, col:col + width]), preferred_element_type=jnp.float32)
                + b_in_ref[:, col:col + width])

    def sub_tile(s):
        r0 = s * sub
        slot = s % vn_ref.shape[0]
        ext_rows = slice(0 if s == 0 else r0 + 2 * HALO, r0 + sub + 2 * HALO)
        if s == 0:
            xext_ref[0:HALO, :] = xprev_ref[...].astype(jnp.bfloat16)
        own = slice(max(ext_rows.start - HALO, 0), min(ext_rows.stop - HALO, tm))
        xext_ref[HALO + own.start:HALO + own.stop, :] = x_ref[own, :].astype(jnp.bfloat16)
        if s == n_sub - 1:
            xext_ref[HALO + tm:, :] = xnext_ref[...].astype(jnp.bfloat16)
        def xb():
            return xext_ref[HALO + r0:HALO + r0 + sub, :]

        v = in_proj(xb(), COL_V)
        vn_ref[slot] = _layernorm(v, ln_v_g_ref[...], ln_v_b_ref[...]).astype(jnp.bfloat16)
        yield

        pext_ref[ext_rows, :] = (jnp.dot(xext_ref[ext_rows, :], wpp_ref[...], preferred_element_type=jnp.float32)
                                 + bpp_ref[0:1, :]).astype(jnp.bfloat16)
        for n in range(0, n_chunks, 2):
            rows0 = slice(n * CHUNK, (n + 1) * CHUNK)
            rows1 = slice((n + 1) * CHUNK, (n + 2) * CHUNK)
            for g in range(A_GROUPS):
                cols = slice(g * A_GROUP_W, (g + 1) * A_GROUP_W)
                rhs = jnp.concatenate([vn_ref[slot, rows0, cols], vn_ref[slot, rows1, cols]], axis=1)
                mixed = jnp.dot(_as_bf16(w_sp_ref[g * CHUNK // 2:(g + 1) * CHUNK // 2, :]), rhs,
                                preferred_element_type=jnp.float32)
                vs_ref[slot, rows0, cols] = mixed[:, :A_GROUP_W] + b_sp_ref[:, cols]
                vs_ref[slot, rows1, cols] = mixed[:, A_GROUP_W:] + b_sp_ref[:, cols]
        for c in range(n_chunks):
            variant = jnp.where(jnp.logical_and(is_first, s == 0 and c == 0), 0,
                                jnp.where(jnp.logical_and(is_last, s == n_sub - 1 and c == n_chunks - 1), 2, 1))
            for g in range(len(POOL_WINDOWS)):
                cols = slice(g * POOL_GROUP_W, (g + 1) * POOL_GROUP_W)
                m_ref[slot, c * CHUNK:(c + 1) * CHUNK, cols] = (
                    jnp.dot(_as_bf16(band_ref[variant, g]), pext_ref[r0 + c * CHUNK:r0 + c * CHUNK + BAND_K, cols],
                            preferred_element_type=jnp.float32) + b_pool_ref[:, cols])

        for n in range(D_MODEL // COL_BLOCK):
            cols = slice(n * COL_BLOCK, (n + 1) * COL_BLOCK)
            z_a = in_proj(xb(), COL_ZA + n * COL_BLOCK, COL_BLOCK)
            u = in_proj(xb(), COL_U + n * COL_BLOCK, COL_BLOCK)
            bra_ref[slot, :, cols] = (u * vs_ref[slot, :, cols] * _silu(z_a)).astype(jnp.bfloat16)
            z_b = in_proj(xb(), COL_ZB + n * COL_BLOCK, COL_BLOCK)
            brb_ref[slot, :, cols] = (m_ref[slot, :, cols] * pool_scale_ref[:, cols] * _silu(z_b)).astype(jnp.bfloat16)
        resid = DEEPNORM_ALPHA * x_ref[r0:r0 + sub, :] + b_out_ref[...]
        for n in range(D_MODEL // COL_BLOCK):
            cols = slice(n * COL_BLOCK, (n + 1) * COL_BLOCK)
            g_a = in_proj(xb(), COL_GA + n * COL_BLOCK, COL_BLOCK)
            proj_a = jnp.dot(bra_ref[slot], _as_bf16(w_br_a_ref[:, cols]), preferred_element_type=jnp.float32)
            g_b = in_proj(xb(), COL_GB + n * COL_BLOCK, COL_BLOCK)
            proj_b = jnp.dot(brb_ref[slot], _as_bf16(w_br_b_ref[:, cols]), preferred_element_type=jnp.float32)
            merged_ref[slot, :, cols] = (jax.nn.sigmoid(g_a) * proj_a
                                         + jax.nn.sigmoid(g_b) * proj_b).astype(jnp.bfloat16)
        yield

        out = jnp.dot(merged_ref[slot], _as_bf16(w_out_ref[...]), preferred_element_type=jnp.float32)
        out_ref[r0:r0 + sub, :] = _layernorm(resid + out, ln_g_ref[...], ln_b_ref[...])
        yield

    tiles = [sub_tile(s) for s in range(n_sub)]
    next(tiles[0])
    for s in range(n_sub):
        next(tiles[s])
        if s + 1 < n_sub:
            next(tiles[s + 1])
        next(tiles[s])


def _pool_band_table():
    t = np.arange(CHUNK)[:, None]
    k = np.arange(BAND_K)[None, :] - HALO
    table = np.zeros((3, len(POOL_WINDOWS), CHUNK, BAND_K), np.float32)
    for variant, valid in enumerate((k >= 0, np.ones_like(k, bool), k < CHUNK)):
        for g, w in enumerate(POOL_WINDOWS):
            in_window = (k >= t - w // 2) & (k < t + w - w // 2) & valid
            table[variant, g] = in_window / in_window.sum(axis=1, keepdims=True) - (k == t)
    bits = table.astype(jnp.bfloat16).view(np.uint16).astype(np.uint32)
    return jnp.asarray(bits[..., 0::2, :] | (bits[..., 1::2, :] << 16))


N_WEIGHTS = 6
N_SMALL = 10


def _layers_kernel(*refs, seq_lens, tm):
    n_streams = len(seq_lens)
    refs = list(refs)
    take = lambda n: [refs.pop(0) for _ in range(n)]
    xs, w_hbm, small, ys, wq = take(n_streams), take(N_WEIGHTS), take(N_SMALL), take(n_streams), take(N_WEIGHTS)
    step_ref = refs.pop()
    scratch = refs
    w_in, w_sp, w_pool, w_br_a, w_br_b, w_out = wq
    b_in, ln_v_g, ln_v_b, b_sp, b_pool, pool_scale, b_out, ln_g, ln_b, band = small
    params = (w_in, b_in, ln_v_g, ln_v_b, w_sp, b_sp, w_pool, b_pool, pool_scale, w_br_a, w_br_b, w_out,
              b_out, ln_g, ln_b, band)

    step_ref[0] = 0

    def cast_block(*blocks):
        i = step_ref[0]
        step_ref[0] = i + 1
        packed_rows = CAST_BLOCK_ROWS // 2
        rows = pl.ds(pl.multiple_of(i * packed_rows, packed_rows), packed_rows)
        for blk, dst in zip(blocks, wq):
            dst[rows, :] = pltpu.bitcast(blk[...].astype(jnp.bfloat16), jnp.uint32)

    pltpu.emit_pipeline(
        cast_block,
        grid=(D_MODEL // CAST_BLOCK_ROWS,),
        in_specs=[pl.BlockSpec((CAST_BLOCK_ROWS, w.shape[1]), lambda i: (i, 0)) for w in w_hbm],
        out_specs=[],
    )(*w_hbm)
    wpp_ref, bpp_ref = scratch[5], scratch[6]
    _fold_pool_linear(w_in, b_in, w_pool, wpp_ref, bpp_ref)

    d = D_MODEL
    halo_per_tile = tm // HALO
    for x2, y2, seq_len in zip(xs, ys, seq_lens):
        n_tok = x2.shape[0]
        last_halo_block = n_tok // HALO - 1
        step_ref[0] = 0

        def step(xprev_ref, x_ref, xnext_ref, out_ref, seq_len=seq_len):
            tile = step_ref[0]
            step_ref[0] = tile + 1
            _layer_step(tile, seq_len // tm, xprev_ref, x_ref, xnext_ref, out_ref, *params, *scratch)

        pltpu.emit_pipeline(
            step,
            grid=(n_tok // tm,),
            in_specs=[
                pl.BlockSpec((HALO, d), lambda i: (jnp.maximum(i * halo_per_tile - 1, 0), 0)),
                pl.BlockSpec((tm, d), lambda i: (i, 0)),
                pl.BlockSpec((HALO, d), lambda i, last=last_halo_block: (
                    jnp.minimum((i + 1) * halo_per_tile, last), 0)),
            ],
            out_specs=[pl.BlockSpec((tm, d), lambda i: (i, 0))],
        )(x2, x2, x2, y2)


def _run_layers(streams, weights, small, tm=TOKEN_TILE, sub=SUB_TILE):
    d = D_MODEL
    assert tm % sub == 0 and sub % (2 * CHUNK) == 0
    assert len(weights) == N_WEIGHTS and len(small) == N_SMALL
    assert all(w.ndim == 2 and w.shape[0] == d for w in weights)
    for x in streams:
        assert x.shape[2] == d and x.shape[1] % tm == 0
        assert x.shape[1] >= 2 * CHUNK
    scratch = [pltpu.VMEM((w.shape[0] // 2, w.shape[1]), jnp.uint32) for w in weights] + [
        pltpu.VMEM((tm + 2 * HALO, d), jnp.bfloat16),
        pltpu.VMEM((SCRATCH_SLOTS, sub, d), jnp.bfloat16),
        pltpu.VMEM((SCRATCH_SLOTS, sub, d), jnp.float32),
        pltpu.VMEM((tm + 2 * HALO, d), jnp.bfloat16),
        pltpu.VMEM((SCRATCH_SLOTS, sub, d), jnp.float32),
        pltpu.VMEM((d, d), jnp.bfloat16),
        pltpu.VMEM((HALO, d), jnp.float32),
        pltpu.VMEM((SCRATCH_SLOTS, sub, d), jnp.bfloat16),
        pltpu.VMEM((SCRATCH_SLOTS, sub, d), jnp.bfloat16),
        pltpu.VMEM((SCRATCH_SLOTS, sub, d), jnp.bfloat16),
        pltpu.SMEM((1,), jnp.int32),
    ]
    in_hbm = pl.BlockSpec(memory_space=pl.ANY)
    in_vmem = pl.BlockSpec(memory_space=pltpu.VMEM)
    flat = [x.reshape(-1, d) for x in streams]
    ys = pl.pallas_call(
        functools.partial(_layers_kernel, seq_lens=tuple(x.shape[1] for x in streams), tm=tm),
        in_specs=[in_hbm] * (len(streams) + N_WEIGHTS) + [in_vmem] * N_SMALL,
        out_specs=[in_hbm] * len(streams),
        out_shape=[jax.ShapeDtypeStruct(f.shape, f.dtype) for f in flat],
        scratch_shapes=scratch,
        compiler_params=pltpu.CompilerParams(vmem_limit_bytes=VMEM_LIMIT_BYTES),
    )(*flat, *weights, *small)
    return [y.reshape(x.shape) for y, x in zip(ys, streams)]


def kernel(x_prompt, x_sample, w_in, b_in, ln_v_g, ln_v_b, w_spatial, b_spatial, w_pool, b_pool, pool_scale,
           w_br_a, w_br_b, w_out, b_out, ln_g, ln_b):
    assert w_in.shape[0] == DEPTH
    row = lambda a: a[0].reshape(1, -1)
    b_sp = jnp.repeat(b_spatial[0].T, A_GROUP_W, axis=1)
    weights = (w_in[0], w_spatial[0].reshape(A_GROUPS * CHUNK, CHUNK),
               w_pool[0].reshape(len(POOL_WINDOWS) * POOL_GROUP_W, POOL_GROUP_W), w_br_a[0], w_br_b[0], w_out[0])
    small = (row(b_in), row(ln_v_g), row(ln_v_b), b_sp, row(b_pool), row(pool_scale), row(b_out), row(ln_g),
             row(ln_b), _pool_band_table())
    y_prompt, y_sample = _run_layers((x_prompt, x_sample), weights, small)
    return y_prompt, y_sample
```
